```python
import jax, jax.numpy as jnp
from jax import lax
import numpy as np

D_MODEL = 1024
BATCH = 8
SEQ = 2048
DEPTH = 2
DEC_BATCH = 128
DEC_SEQ = 8
PAST_LEN = 2048
PAGE_SIZE = 128

ATT_HEADS = 8
HEAD_DIM = 64
ATT_W = ATT_HEADS * HEAD_DIM
CONV_W = D_MODEL // 2
CONV_K = 3
POOL_WINDOWS = (2, 4, 8, 16)
POOL_GROUPS = len(POOL_WINDOWS)
POOL_GW = D_MODEL // 8
POOL_W = POOL_GROUPS * POOL_GW
POOL_STATE = max(POOL_WINDOWS) - 1
N_BRANCH = 3
FF = -(-8 * D_MODEL // (3 * 256)) * 256
Q_BLOCK = 128
ALPHA = (2 * DEPTH) ** 0.25
BETA = (8 * DEPTH) ** -0.25
LN_EPS = 1e-5
FORGET_BIAS = 3.0

OFF_Q = 0
OFF_K = OFF_Q + ATT_W
OFF_V = OFF_K + ATT_W
OFF_F = OFF_V + ATT_W
OFF_CB = OFF_F + ATT_HEADS
OFF_CC = OFF_CB + CONV_W
OFF_CH = OFF_CC + CONV_W
OFF_P = OFF_CH + CONV_W
OFF_G = OFF_P + POOL_W
IN_W = OFF_G + N_BRANCH * D_MODEL

kernel_name = "hybrid_conv_pool_forgetting_attn_decoder_step"


def layer_norm(x, g, b):
    xf = x.astype(jnp.float32)
    mu = jnp.mean(xf, axis=-1, keepdims=True)
    var = jnp.mean(jnp.square(xf - mu), axis=-1, keepdims=True)
    return ((xf - mu) * lax.rsqrt(var + LN_EPS) * g + b).astype(x.dtype)


def forget_attention(q, k, v, dq, dk, q_pos, k_pos):
    s = jnp.einsum('bqhd,bkhd->bhqk', q, k).astype(jnp.float32) * (HEAD_DIM ** -0.5)
    s = s + jnp.transpose(dq, (0, 2, 1))[..., None] - jnp.transpose(dk, (0, 2, 1))[:, :, None, :]
    mask = k_pos[None, :] <= q_pos[:, None]
    s = jnp.where(mask, s, -jnp.inf)
    p = jax.nn.softmax(s, axis=-1)
    return jnp.einsum('bhqk,bkhd->bqhd', p.astype(v.dtype), v)


def causal_forget_attention(q, k, v, dq, dk, q_pos, k_pos):
    B, T = q.shape[0], q.shape[1]
    if T <= Q_BLOCK or T % Q_BLOCK:
        return forget_attention(q, k, v, dq, dk, q_pos, k_pos)
    nb = T // Q_BLOCK
    qb = q.reshape(B, nb, Q_BLOCK, ATT_HEADS, HEAD_DIM).swapaxes(0, 1)
    dqb = dq.reshape(B, nb, Q_BLOCK, ATT_HEADS).swapaxes(0, 1)
    pb = q_pos.reshape(nb, Q_BLOCK)
    out = lax.map(lambda a: forget_attention(a[0], k, v, a[1], dk, a[2], k_pos), (qb, dqb, pb))
    return out.swapaxes(0, 1).reshape(B, T, ATT_HEADS, HEAD_DIM)


def token_mixers(h, conv_prev, pool_prev, k_past, v_past, lf_past, lp):
    B, T, _ = h.shape
    P = k_past.shape[1]
    z = h @ lp['w_in']
    pos = P + jnp.arange(T)

    q = z[..., OFF_Q:OFF_K].reshape(B, T, ATT_HEADS, HEAD_DIM)
    k = z[..., OFF_K:OFF_V].reshape(B, T, ATT_HEADS, HEAD_DIM)
    v = z[..., OFF_V:OFF_F].reshape(B, T, ATT_HEADS, HEAD_DIM)
    lf = jax.nn.log_sigmoid((z[..., OFF_F:OFF_CB] + lp['b_forget']).astype(jnp.float32))
    k_all = jnp.concatenate([k_past.astype(k.dtype), k], axis=1)
    v_all = jnp.concatenate([v_past.astype(v.dtype), v], axis=1)
    d_all = jnp.cumsum(jnp.concatenate([lf_past.astype(jnp.float32), lf], axis=1), axis=1)
    k_pos = jnp.arange(P + T)
    att = causal_forget_attention(q, k_all, v_all, d_all[:, P:], d_all, pos, k_pos).reshape(B, T, ATT_W)

    gb = z[..., OFF_CB:OFF_CC]
    gc = z[..., OFF_CC:OFF_CH]
    u = gc * z[..., OFF_CH:OFF_P]
    u_ext = jnp.concatenate([conv_prev.astype(u.dtype), u], axis=1)
    cw = lp['conv_w']
    y = cw[0] * u_ext[:, 0:T]
    for j in range(1, CONV_K):
        y = y + cw[j] * u_ext[:, j:j + T]
    conv_out = gb * y

    p = z[..., OFF_P:OFF_G]
    p_ext = jnp.concatenate([pool_prev.astype(p.dtype), p], axis=1)
    cs = jnp.cumsum(jnp.pad(p_ext.astype(jnp.float32), ((0, 0), (1, 0), (0, 0))), axis=1)
    end = cs[:, POOL_STATE + 1:]
    means = []
    for g, w in enumerate(POOL_WINDOWS):
        sl = slice(g * POOL_GW, (g + 1) * POOL_GW)
        start = cs[:, POOL_STATE + 1 - w:POOL_STATE + 1 - w + T, sl]
        cnt = jnp.minimum(pos + 1, w).astype(jnp.float32)[None, :, None]
        means.append((end[..., sl] - start) / cnt)
    pooled = jnp.concatenate(means, axis=-1) - p.astype(jnp.float32)
    pooled = pooled.reshape(B, T, POOL_GROUPS, POOL_GW).astype(h.dtype)
    pool_out = jnp.einsum('btgc,gcd->btgd', pooled, lp['pool_w']).reshape(B, T, POOL_W) * lp['pool_scale']

    gates = jax.nn.sigmoid(z[..., OFF_G:].astype(jnp.float32)).astype(h.dtype).reshape(B, T, N_BRANCH, D_MODEL)
    merged = (gates[:, :, 0] * (att @ lp['w_br_attn'])
              + gates[:, :, 1] * (conv_out @ lp['w_br_conv'])
              + gates[:, :, 2] * (pool_out @ lp['w_br_pool']))
    out = merged @ lp['w_o']
    new_state = (k, v, lf, u_ext[:, -(CONV_K - 1):], p_ext[:, -POOL_STATE:])
    return out, new_state


def decoder_layer(x, c, conv_prev, pool_prev, k_past, v_past, lf_past, lp):
    mod = (jax.nn.silu(c) @ lp['ada_w'] + lp['ada_b'])[:, None, :]
    sh1, sc1, g1, sh2, sc2, g2 = jnp.split(mod, 6, axis=-1)
    mix, st = token_mixers(x * (1 + sc1) + sh1, conv_prev, pool_prev, k_past, v_past, lf_past, lp)
    x = layer_norm(ALPHA * x + g1 * mix, lp['ln1_g'], lp['ln1_b'])
    h2 = x * (1 + sc2) + sh2
    a, b = jnp.split(h2 @ lp['w_gate_up'], 2, axis=-1)
    f = (jax.nn.silu(a) * b) @ lp['w_down']
    x = layer_norm(ALPHA * x + g2 * f, lp['ln2_g'], lp['ln2_b'])
    return x, st


def setup_inputs(seed: int = 0) -> dict:
    key = jax.random.key(seed)
    ks = jax.random.split(key, 32)

    def nrm(i, shape, scale):
        return jax.random.normal(ks[i], shape, jnp.float32) * scale

    n_pages = PAST_LEN // PAGE_SIZE
    n_used = DEC_BATCH * n_pages
    n_pool = n_used + max(1, n_used // 4)
    D = D_MODEL
    return {
        'x_prompt': nrm(0, (BATCH, SEQ, D), 1.0),
        'x_sample': nrm(1, (DEC_BATCH, DEC_SEQ, D), 1.0),
        'cache_k': nrm(2, (DEPTH, n_pool, PAGE_SIZE, ATT_HEADS, HEAD_DIM), 1.0),
        'cache_v': nrm(3, (DEPTH, n_pool, PAGE_SIZE, ATT_HEADS, HEAD_DIM), 1.0),
        'cache_logf': jax.nn.log_sigmoid(FORGET_BIAS + nrm(4, (DEPTH, n_pool, PAGE_SIZE, ATT_HEADS), 1.0)),
        'state_conv': nrm(5, (DEPTH, DEC_BATCH, CONV_K - 1, CONV_W), 1.0),
        'state_pool': nrm(6, (DEPTH, DEC_BATCH, POOL_STATE, POOL_W), 1.0),
        'page_table': jax.random.permutation(ks[7], n_pool)[:n_used].reshape(DEC_BATCH, n_pages).astype(jnp.int32),
        'c_prompt': nrm(8, (BATCH, D), 1.0),
        'c_sample': nrm(9, (DEC_BATCH, D), 1.0),
        'ada_w': nrm(10, (DEPTH, D, 6 * D), 0.5 * D ** -0.5),
        'ada_b': nrm(11, (DEPTH, 6 * D), 0.02),
        'w_in': nrm(12, (DEPTH, D, IN_W), D ** -0.5),
        'b_forget': FORGET_BIAS + nrm(13, (DEPTH, ATT_HEADS), 0.1),
        'conv_w': nrm(14, (DEPTH, CONV_K, CONV_W), CONV_K ** -0.5),
        'pool_w': nrm(15, (DEPTH, POOL_GROUPS, POOL_GW, POOL_GW), POOL_GW ** -0.5),
        'pool_scale': 1.0 + nrm(16, (DEPTH, POOL_W), 0.1),
        'w_br_attn': nrm(17, (DEPTH, ATT_W, D), ATT_W ** -0.5),
        'w_br_conv': nrm(18, (DEPTH, CONV_W, D), CONV_W ** -0.5),
        'w_br_pool': nrm(19, (DEPTH, POOL_W, D), POOL_W ** -0.5),
        'w_o': nrm(20, (DEPTH, D, D), BETA * D ** -0.5),
        'ln1_g': 1.0 + nrm(21, (DEPTH, D), 0.05),
        'ln1_b': nrm(22, (DEPTH, D), 0.02),
        'w_gate_up': nrm(23, (DEPTH, D, 2 * FF), D ** -0.5),
        'w_down': nrm(24, (DEPTH, FF, D), BETA * FF ** -0.5),
        'ln2_g': 1.0 + nrm(25, (DEPTH, D), 0.05),
        'ln2_b': nrm(26, (DEPTH, D), 0.02),
    }


def reference(x_prompt, x_sample, cache_k, cache_v, cache_logf, state_conv, state_pool, page_table,
              c_prompt, c_sample, ada_w, ada_b, w_in, b_forget, conv_w, pool_w, pool_scale,
              w_br_attn, w_br_conv, w_br_pool, w_o, ln1_g, ln1_b, w_gate_up, w_down, ln2_g, ln2_b):
    xp, xs = x_prompt, x_sample
    Bp, Bs = xp.shape[0], xs.shape[0]
    empty_kv = jnp.zeros((Bp, 0, ATT_HEADS, HEAD_DIM), xp.dtype)
    empty_lf = jnp.zeros((Bp, 0, ATT_HEADS), jnp.float32)
    conv0 = jnp.zeros((Bp, CONV_K - 1, CONV_W), xp.dtype)
    pool0 = jnp.zeros((Bp, POOL_STATE, POOL_W), xp.dtype)
    sp = ([], [], [], [], [])
    ss = ([], [], [], [], [])
    for l in range(DEPTH):
        lp = dict(ada_w=ada_w[l], ada_b=ada_b[l], w_in=w_in[l], b_forget=b_forget[l], conv_w=conv_w[l],
                  pool_w=pool_w[l], pool_scale=pool_scale[l], w_br_attn=w_br_attn[l], w_br_conv=w_br_conv[l],
                  w_br_pool=w_br_pool[l], w_o=w_o[l], ln1_g=ln1_g[l], ln1_b=ln1_b[l],
                  w_gate_up=w_gate_up[l], w_down=w_down[l], ln2_g=ln2_g[l], ln2_b=ln2_b[l])
        xp, st_p = decoder_layer(xp, c_prompt, conv0, pool0, empty_kv, empty_kv, empty_lf, lp)
        k_past = cache_k[l][page_table].reshape(Bs, -1, ATT_HEADS, HEAD_DIM)
        v_past = cache_v[l][page_table].reshape(Bs, -1, ATT_HEADS, HEAD_DIM)
        lf_past = cache_logf[l][page_table].reshape(Bs, -1, ATT_HEADS)
        xs, st_s = decoder_layer(xs, c_sample, state_conv[l], state_pool[l], k_past, v_past, lf_past, lp)
        for i in range(5):
            sp[i].append(st_p[i])
            ss[i].append(st_s[i])
    k_p, v_p, lf_p, conv_p, pool_p = [jnp.stack(a) for a in sp]
    k_s, v_s, lf_s, conv_s, pool_s = [jnp.stack(a) for a in ss]
    return (xp, xs, k_p, v_p, lf_p, conv_p, pool_p, k_s, v_s, lf_s, conv_s, pool_s)
```

```python
import functools

import jax
import jax.numpy as jnp
from jax import lax
from jax.experimental import pallas as pl
from jax.experimental.pallas import tpu as pltpu

F32 = jnp.float32
BF16 = jnp.bfloat16

D_MODEL = 1024
N_HEADS = 8
HEAD_DIM = 64
ATT_W = N_HEADS * HEAD_DIM
CONV_W = 512
CONV_K = 3
POOL_WINDOWS = (2, 4, 8, 16)
POOL_GW = 128
POOL_W = len(POOL_WINDOWS) * POOL_GW
POOL_STATE = max(POOL_WINDOWS) - 1
POOL_HIST = POOL_STATE + 1
CONV_HIST = 8
FF = 2816
FF_CHUNK = 1408
LN_EPS = 1e-5
PAGE = 128
LANES = 128
NEG = -1e30

OFF_Q = 0
OFF_F = 3 * ATT_W
OFF_CB = OFF_F + N_HEADS
OFF_P = OFF_CB + 3 * CONV_W
OFF_G = OFF_P + POOL_W

VMEM_LIMIT = 58 * 1024 * 1024


def _dot(a, b):
    return jnp.dot(a, b, preferred_element_type=F32)


def _dot_nt(a, b):
    return lax.dot_general(a, b, (((1,), (1,)), ((), ())), preferred_element_type=F32)


def _dot_hi(a, b):
    return jnp.dot(a, b, preferred_element_type=F32, precision=lax.Precision.HIGHEST)


def _log_sigmoid(x):
    return jnp.minimum(x, 0.0) - jnp.log1p(jnp.exp(-jnp.abs(x)))


def _silu(x):
    return x * jax.nn.sigmoid(x)


def _layer_norm(x, g, b):
    mu = jnp.mean(x, axis=-1, keepdims=True)
    xc = x - mu
    var = jnp.mean(xc * xc, axis=-1, keepdims=True)
    return xc * lax.rsqrt(var + LN_EPS) * g + b


def _const_spec(shape):
    nd = len(shape)
    return pl.BlockSpec(shape, lambda *_: (0,) * nd, pipeline_mode=pl.Buffered(1))


def _mod_kernel(c_ref, w_ref, b_ref, o_ref):
    c = c_ref[...]
    o_ref[0] = _dot(_silu(c).astype(BF16), w_ref[0].astype(BF16)) + b_ref[0]


def _modulation(c_all, ada_w, ada_b):
    depth, d, n = ada_w.shape
    rows = c_all.shape[0]
    tn = 1536
    return pl.pallas_call(
        _mod_kernel,
        out_shape=jax.ShapeDtypeStruct((depth, rows, n), F32),
        grid=(depth, n // tn),
        in_specs=[pl.BlockSpec((rows, d), lambda l, j: (0, 0)),
                  pl.BlockSpec((1, d, tn), lambda l, j: (l, 0, j)),
                  pl.BlockSpec((1, 1, tn), lambda l, j: (l, 0, j))],
        out_specs=pl.BlockSpec((1, rows, tn), lambda l, j: (l, 0, j)),
        compiler_params=pltpu.CompilerParams(vmem_limit_bytes=VMEM_LIMIT),
        name="modulation",
    )(c_all, ada_w, ada_b.reshape(depth, 1, n))


def _gated_merge_partial(h, conv_out, pool_out, wg_ref, wbc_ref, wbp_ref):
    d = D_MODEL
    g_att = jax.nn.sigmoid(_dot(h, wg_ref[:, 0:d]))
    g_conv = jax.nn.sigmoid(_dot(h, wg_ref[:, d:2 * d]))
    g_pool = jax.nn.sigmoid(_dot(h, wg_ref[:, 2 * d:3 * d]))
    pm = g_conv * _dot(conv_out, wbc_ref[...]) + g_pool * _dot(pool_out, wbp_ref[...])
    return g_att, pm


def _front_prompt_kernel(x_ref, mod_ref, wqkv_ref, wf_ref, bf_ref, wc_ref, cw_ref, wp_ref, pw_ref, ps_ref,
                         wg_ref, wbc_ref, wbp_ref,
                         q_ref, kt_ref, vt_ref, lft_ref, dcol_ref, drow_ref, g0_ref, pm_ref, cst_ref, pst_ref,
                         ucar, pcar, dcar, *, tm):
    d = D_MODEL
    ti = pl.program_id(1)

    @pl.when(ti == 0)
    def _():
        ucar[...] = jnp.zeros_like(ucar)
        pcar[...] = jnp.zeros_like(pcar)
        dcar[...] = jnp.zeros_like(dcar)

    x = x_ref[0]
    sh = mod_ref[0, :, 0:d]
    sc = mod_ref[0, :, d:2 * d]
    h = (x * (1.0 + sc) + sh).astype(BF16)

    qkv = _dot(h, wqkv_ref[...])
    q_ref[0] = qkv[:, 0:ATT_W]
    kt_ref[0] = qkv[:, ATT_W:2 * ATT_W].T
    vt_ref[0] = qkv[:, 2 * ATT_W:3 * ATT_W].T

    lf = _log_sigmoid(_dot(h, wf_ref[...]) + bf_ref[...])
    r = lax.broadcasted_iota(jnp.int32, (tm, tm), 0)
    c = lax.broadcasted_iota(jnp.int32, (tm, tm), 1)
    tri = jnp.where(c <= r, 1.0, 0.0).astype(F32)
    dcum = _dot_hi(tri, lf) + dcar[...]
    dcar[...] = dcum[tm - 1:tm, :]
    lft_ref[0] = lf.T[0:N_HEADS, :]
    dcol_ref[0] = dcum[:, 0:N_HEADS]
    drow_ref[0] = dcum.T[0:N_HEADS, :]

    zc = _dot(h, wc_ref[...])
    gb = zc[:, 0:CONV_W]
    u = zc[:, CONV_W:2 * CONV_W] * zc[:, 2 * CONV_W:3 * CONV_W]
    uext = jnp.concatenate([ucar[...], u], axis=0)
    u1 = pltpu.roll(uext, 1, axis=0)[CONV_HIST:, :]
    u2 = pltpu.roll(uext, 2, axis=0)[CONV_HIST:, :]
    y = cw_ref[0:1, :] * u2 + cw_ref[1:2, :] * u1 + cw_ref[2:3, :] * u
    conv_out = (gb * y).astype(BF16)
    ucar[...] = u[tm - CONV_HIST:, :]
    cst_ref[0] = u[tm - CONV_HIST:, :]

    p = _dot(h, wp_ref[...])
    pext = jnp.concatenate([pcar[...], p], axis=0)
    pos1 = ti * tm + lax.broadcasted_iota(jnp.int32, (tm, POOL_GW), 0) + 1
    pouts = []
    for g, w in enumerate(POOL_WINDOWS):
        cols = slice(g * POOL_GW, (g + 1) * POOL_GW)
        s = pext[:, cols]
        k = 1
        while k < w:
            s = s + pltpu.roll(s, k, axis=0)
            k *= 2
        cnt = jnp.minimum(pos1, w).astype(F32)
        pooled = (s[POOL_HIST:, :] / cnt - p[:, cols]).astype(BF16)
        pouts.append(_dot(pooled, pw_ref[g]))
    pool_out = (jnp.concatenate(pouts, axis=1) * ps_ref[...]).astype(BF16)
    pcar[...] = p[tm - POOL_HIST:, :]
    pst_ref[0] = p[tm - POOL_HIST:, :]

    g_att, pm = _gated_merge_partial(h, conv_out, pool_out, wg_ref, wbc_ref, wbp_ref)
    g0_ref[0] = g_att
    pm_ref[0] = pm


def _front_prompt(x, mod, w, *, tm):
    b, t, d = x.shape
    nt = t // tm
    tile = lambda width: pl.BlockSpec((1, tm, width), lambda i, j: (i, j, 0))
    tile_t = lambda rows: pl.BlockSpec((1, rows, tm), lambda i, j: (i, 0, j))
    per_b = lambda rows, width: pl.BlockSpec((1, rows, width), lambda i, j: (i, 0, 0))
    weights = (w["wqkv"], w["wf"], w["bf"], w["wc"], w["cw"], w["wp"], w["pw"], w["ps"], w["wg"], w["wbc"], w["wbp"])
    return pl.pallas_call(
        functools.partial(_front_prompt_kernel, tm=tm),
        out_shape=(jax.ShapeDtypeStruct((b, t, ATT_W), F32),
                   jax.ShapeDtypeStruct((b, ATT_W, t), F32),
                   jax.ShapeDtypeStruct((b, ATT_W, t), F32),
                   jax.ShapeDtypeStruct((b, N_HEADS, t), F32),
                   jax.ShapeDtypeStruct((b, t, N_HEADS), F32),
                   jax.ShapeDtypeStruct((b, N_HEADS, t), F32),
                   jax.ShapeDtypeStruct((b, t, d), F32),
                   jax.ShapeDtypeStruct((b, t, d), F32),
                   jax.ShapeDtypeStruct((b, CONV_HIST, CONV_W), F32),
                   jax.ShapeDtypeStruct((b, POOL_HIST, POOL_W), F32)),
        grid=(b, nt),
        in_specs=[tile(d), pl.BlockSpec((1, 1, 6 * d), lambda i, j: (i, 0, 0))] + [_const_spec(a.shape) for a in weights],
        out_specs=(tile(ATT_W), tile_t(ATT_W), tile_t(ATT_W), tile_t(N_HEADS), tile(N_HEADS), tile_t(N_HEADS),
                   tile(d), tile(d), per_b(CONV_HIST, CONV_W), per_b(POOL_HIST, POOL_W)),
        scratch_shapes=[pltpu.VMEM((CONV_HIST, CONV_W), F32), pltpu.VMEM((POOL_HIST, POOL_W), F32),
                        pltpu.VMEM((1, LANES), F32)],
        compiler_params=pltpu.CompilerParams(dimension_semantics=("parallel", "arbitrary"),
                                             vmem_limit_bytes=VMEM_LIMIT),
        name="front_prompt",
    )(x, mod, *weights)


def _attn_prompt_kernel(q_ref, kt_ref, vt_ref, dcol_ref, drow_ref, o_ref, *, tq):
    qi = pl.program_id(1)
    row = lax.broadcasted_iota(jnp.int32, (tq, tq), 0)
    col = lax.broadcasted_iota(jnp.int32, (tq, tq), 1)
    causal = col <= row
    k0_diag = pl.multiple_of(qi * tq, tq)
    outs = []
    for hd in range(N_HEADS):
        hrows = slice(hd * HEAD_DIM, (hd + 1) * HEAD_DIM)
        qh = (q_ref[0, :, hrows] * (HEAD_DIM ** -0.5)).astype(BF16)
        dq = dcol_ref[0, :, hd:hd + 1]

        def scores(k0, hrows=hrows, hd=hd, qh=qh, dq=dq):
            kt = kt_ref[0, hrows, pl.ds(k0, tq)].astype(BF16)
            dk = drow_ref[0, hd:hd + 1, pl.ds(k0, tq)]
            return _dot(qh, kt) + dq - dk

        def pv(p, k0, hrows=hrows):
            vt = vt_ref[0, hrows, pl.ds(k0, tq)].astype(BF16)
            return _dot_nt(p.astype(BF16), vt)

        s = jnp.where(causal, scores(k0_diag), NEG)
        m = jnp.max(s, axis=1, keepdims=True)
        p = jnp.exp(s - m)
        l = jnp.sum(p, axis=1, keepdims=True)
        acc = pv(p, k0_diag)

        def body(j, carry, scores=scores, pv=pv):
            m, l, acc = carry
            k0 = pl.multiple_of(j * tq, tq)
            s = scores(k0)
            m_new = jnp.maximum(m, jnp.max(s, axis=1, keepdims=True))
            a = jnp.exp(m - m_new)
            p = jnp.exp(s - m_new)
            return m_new, a * l + jnp.sum(p, axis=1, keepdims=True), a * acc + pv(p, k0)

        m, l, acc = lax.fori_loop(0, qi, body, (m, l, acc))
        outs.append(acc / l)
    o_ref[0] = jnp.concatenate(outs, axis=1)


def _attn_prompt(q, kt, vt, dcol, drow, *, tq):
    b, t, _ = q.shape
    return pl.pallas_call(
        functools.partial(_attn_prompt_kernel, tq=tq),
        out_shape=jax.ShapeDtypeStruct((b, t, ATT_W), F32),
        grid=(b, t // tq),
        in_specs=[pl.BlockSpec((1, tq, ATT_W), lambda i, j: (i, j, 0)),
                  pl.BlockSpec((1, ATT_W, t), lambda i, j: (i, 0, 0)),
                  pl.BlockSpec((1, ATT_W, t), lambda i, j: (i, 0, 0)),
                  pl.BlockSpec((1, tq, N_HEADS), lambda i, j: (i, j, 0)),
                  pl.BlockSpec((1, N_HEADS, t), lambda i, j: (i, 0, 0))],
        out_specs=pl.BlockSpec((1, tq, ATT_W), lambda i, j: (i, j, 0)),
        compiler_params=pltpu.CompilerParams(dimension_semantics=("parallel", "arbitrary"),
                                             vmem_limit_bytes=VMEM_LIMIT),
        name="attn_prompt",
    )(q, kt, vt, dcol, drow)


def _back_kernel(x_ref, att_ref, g0_ref, pm_ref, mod_ref, wba_ref, wo_ref, l1g_ref, l1b_ref, wgu_ref, wd_ref,
                 l2g_ref, l2b_ref, o_ref, *, alpha, mod_rep):
    d = D_MODEL

    def mod(k):
        if mod_rep == 0:
            return mod_ref[0, :, k * d:(k + 1) * d]
        m = mod_ref[:, k * d:(k + 1) * d]
        return jnp.concatenate([m] * mod_rep, axis=0)

    x = x_ref[...].reshape(x_ref.shape[-2:])
    att = att_ref[...].reshape(att_ref.shape[-2:])
    g0 = g0_ref[...].reshape(x.shape)
    pm = pm_ref[...].reshape(x.shape)
    merged = g0 * _dot(att.astype(BF16), wba_ref[...]) + pm
    mix = _dot(merged.astype(BF16), wo_ref[...])
    x1 = _layer_norm(alpha * x + mod(2) * mix, l1g_ref[...], l1b_ref[...])
    h2 = (x1 * (1.0 + mod(4)) + mod(3)).astype(BF16)
    f = None
    for c0 in range(0, FF, FF_CHUNK):
        a = _dot(h2, wgu_ref[:, c0:c0 + FF_CHUNK])
        b = _dot(h2, wgu_ref[:, FF + c0:FF + c0 + FF_CHUNK])
        fc = _dot((_silu(a) * b).astype(BF16), wd_ref[c0:c0 + FF_CHUNK, :])
        f = fc if f is None else f + fc
    out = _layer_norm(alpha * x1 + mod(5) * f, l2g_ref[...], l2b_ref[...])
    o_ref[...] = out.reshape(o_ref.shape)


def _back_weights(w):
    return (w["wba"], w["wo"], w["l1g"], w["l1b"], w["wgu"], w["wd"], w["l2g"], w["l2b"])


def _back_prompt(x, att, g0, pm, mod, w, *, tm, alpha):
    b, t, d = x.shape
    tile = lambda width: pl.BlockSpec((1, tm, width), lambda i, j: (i, j, 0))
    weights = _back_weights(w)
    return pl.pallas_call(
        functools.partial(_back_kernel, alpha=alpha, mod_rep=0),
        out_shape=jax.ShapeDtypeStruct((b, t, d), F32),
        grid=(b, t // tm),
        in_specs=[tile(d), tile(ATT_W), tile(d), tile(d), pl.BlockSpec((1, 1, 6 * d), lambda i, j: (i, 0, 0))]
                 + [_const_spec(a.shape) for a in weights],
        out_specs=tile(d),
        compiler_params=pltpu.CompilerParams(dimension_semantics=("parallel", "parallel"),
                                             vmem_limit_bytes=VMEM_LIMIT),
        name="back_prompt",
    )(x, att, g0, pm, mod, *weights)


def _back_sample(x, att, g0, pm, mod, w, *, steps_per_tile, alpha):
    rows, d = x.shape
    n_seq = mod.shape[0]
    tm = steps_per_tile * n_seq
    tile = lambda width: pl.BlockSpec((tm, width), lambda i: (i, 0))
    weights = _back_weights(w)
    return pl.pallas_call(
        functools.partial(_back_kernel, alpha=alpha, mod_rep=steps_per_tile),
        out_shape=jax.ShapeDtypeStruct((rows, d), F32),
        grid=(rows // tm,),
        in_specs=[tile(d), tile(ATT_W), tile(d), tile(d), _const_spec(mod.shape)]
                 + [_const_spec(a.shape) for a in weights],
        out_specs=tile(d),
        compiler_params=pltpu.CompilerParams(dimension_semantics=("parallel",), vmem_limit_bytes=VMEM_LIMIT),
        name="back_sample",
    )(x, att, g0, pm, mod, *weights)


def _front_sample_kernel(x_ref, mod_ref, cprev_ref, pprev_ref, wqkv_ref, wf_ref, bf_ref, wc_ref, cw_ref, wp_ref,
                         pw_ref, ps_ref, wg_ref, wbc_ref, wbp_ref,
                         q_ref, k_ref, v_ref, kt_ref, vt_ref, lft_ref, g0_ref, pm_ref, u_ref, p_ref,
                         ucar, phist, *, past_len):
    d = D_MODEL
    t = pl.program_id(0)

    @pl.when(t == 0)
    def _():
        ucar[...] = cprev_ref[...]
        phist[...] = pprev_ref[...]

    sh = mod_ref[:, 0:d]
    sc = mod_ref[:, d:2 * d]
    h = (x_ref[...] * (1.0 + sc) + sh).astype(BF16)

    qkv = _dot(h, wqkv_ref[...])
    q_ref[...] = qkv[:, 0:ATT_W]
    k_ref[...] = qkv[:, ATT_W:2 * ATT_W]
    v_ref[...] = qkv[:, 2 * ATT_W:3 * ATT_W]
    kt_ref[0] = qkv[:, ATT_W:2 * ATT_W].T
    vt_ref[0] = qkv[:, 2 * ATT_W:3 * ATT_W].T

    lf = _log_sigmoid(_dot(h, wf_ref[...]) + bf_ref[...])
    lft_ref[0] = lf.T[0:N_HEADS, :]

    zc = _dot(h, wc_ref[...])
    gb = zc[:, 0:CONV_W]
    u = zc[:, CONV_W:2 * CONV_W] * zc[:, 2 * CONV_W:3 * CONV_W]
    y = cw_ref[0:1, :] * ucar[0] + cw_ref[1:2, :] * ucar[1] + cw_ref[2:3, :] * u
    conv_out = (gb * y).astype(BF16)
    ucar[0] = ucar[1]
    ucar[1] = u
    u_ref[...] = u

    p = _dot(h, wp_ref[...])
    p_ref[...] = p
    pouts = []
    for g, w in enumerate(POOL_WINDOWS):
        cols = slice(g * POOL_GW, (g + 1) * POOL_GW)
        win = p[:, cols]
        for j in range(1, w):
            win = win + phist[POOL_STATE - j, :, cols]
        cnt = jnp.minimum(past_len + t + 1, w).astype(F32)
        pooled = (win / cnt - p[:, cols]).astype(BF16)
        pouts.append(_dot(pooled, pw_ref[g]))
    pool_out = (jnp.concatenate(pouts, axis=1) * ps_ref[...]).astype(BF16)
    for k in range(POOL_STATE - 1):
        phist[k] = phist[k + 1]
    phist[POOL_STATE - 1] = p

    g_att, pm = _gated_merge_partial(h, conv_out, pool_out, wg_ref, wbc_ref, wbp_ref)
    g0_ref[...] = g_att
    pm_ref[...] = pm


def _front_sample(x, mod, cprev, pprev, w, *, n_t, past_len):
    rows, d = x.shape
    n_seq = rows // n_t
    tile = lambda width: pl.BlockSpec((n_seq, width), lambda i: (i, 0))
    tile_t = lambda r: pl.BlockSpec((1, r, n_seq), lambda i: (i, 0, 0))
    weights = (w["wqkv"], w["wf"], w["bf"], w["wc"], w["cw"], w["wp"], w["pw"], w["ps"], w["wg"], w["wbc"], w["wbp"])
    return pl.pallas_call(
        functools.partial(_front_sample_kernel, past_len=past_len),
        out_shape=(jax.ShapeDtypeStruct((rows, ATT_W), F32),
                   jax.ShapeDtypeStruct((rows, ATT_W), F32),
                   jax.ShapeDtypeStruct((rows, ATT_W), F32),
                   jax.ShapeDtypeStruct((n_t, ATT_W, n_seq), F32),
                   jax.ShapeDtypeStruct((n_t, ATT_W, n_seq), F32),
                   jax.ShapeDtypeStruct((n_t, N_HEADS, n_seq), F32),
                   jax.ShapeDtypeStruct((rows, d), F32),
                   jax.ShapeDtypeStruct((rows, d), F32),
                   jax.ShapeDtypeStruct((rows, CONV_W), F32),
                   jax.ShapeDtypeStruct((rows, POOL_W), F32)),
        grid=(n_t,),
        in_specs=[tile(d), _const_spec(mod.shape), _const_spec(cprev.shape), _const_spec(pprev.shape)]
                 + [_const_spec(a.shape) for a in weights],
        out_specs=(tile(ATT_W), tile(ATT_W), tile(ATT_W), tile_t(ATT_W), tile_t(ATT_W), tile_t(N_HEADS),
                   tile(d), tile(d), tile(CONV_W), tile(POOL_W)),
        scratch_shapes=[pltpu.VMEM((CONV_K - 1, n_seq, CONV_W), F32),
                        pltpu.VMEM((POOL_STATE, n_seq, POOL_W), F32)],
        compiler_params=pltpu.CompilerParams(dimension_semantics=("arbitrary",), vmem_limit_bytes=VMEM_LIMIT),
        name="front_sample",
    )(x, mod, cprev, pprev, *weights)


def _attn_sample_kernel(pt_ref, q_ref, kn_ref, vn_ref, lfn_ref, *rest, n_pages, n_t):
    del pt_ref
    k_refs = rest[0:n_pages]
    v_refs = rest[n_pages:2 * n_pages]
    lf_refs = rest[2 * n_pages:3 * n_pages]
    o_ref = rest[3 * n_pages]
    rows = n_t * N_HEADS

    q = q_ref[0] * (HEAD_DIM ** -0.5)
    head_of_lane = lax.broadcasted_iota(jnp.int32, (N_HEADS, ATT_W), 1) // HEAD_DIM
    head_mask = head_of_lane == lax.broadcasted_iota(jnp.int32, (N_HEADS, ATT_W), 0)
    qbd = jnp.concatenate(
        [jnp.where(head_mask, jnp.broadcast_to(q[t:t + 1, :], (N_HEADS, ATT_W)), 0.0) for t in range(n_t)], axis=0)

    li = lax.broadcasted_iota(jnp.int32, (PAGE, PAGE), 0)
    lj = lax.broadcasted_iota(jnp.int32, (PAGE, PAGE), 1)
    cn = _dot_hi(lfn_ref[0], jnp.where(li <= lj, 1.0, 0.0).astype(F32))
    lf_all = jnp.concatenate([lf_refs[pg][0, 0] for pg in range(n_pages)], axis=0)
    r_in = _dot_hi(lf_all, jnp.where(li > lj, 1.0, 0.0).astype(F32))
    tot = jnp.sum(lf_all, axis=1, keepdims=True)
    carry = jnp.zeros((N_HEADS, 1), F32)
    r_pages = [None] * n_pages
    for pg in reversed(range(n_pages)):
        sl = slice(pg * N_HEADS, (pg + 1) * N_HEADS)
        r_pages[pg] = r_in[sl, :] + carry
        carry = carry + tot[sl, :]
    cn_col = jnp.concatenate([cn[:, t:t + 1] for t in range(n_t)], axis=0)

    s_pages = [_dot(qbd, k_refs[pg][0, 0]) + jnp.concatenate([r_pages[pg]] * n_t, axis=0) + cn_col
               for pg in range(n_pages)]
    pad = jnp.zeros((PAGE - n_t, ATT_W), F32)
    kn = jnp.concatenate([kn_ref[0], pad], axis=0)
    vn = jnp.concatenate([vn_ref[0], pad], axis=0)
    s_new = _dot_nt(qbd, kn) + cn_col - jnp.concatenate([cn] * n_t, axis=0)
    t_of_row = lax.broadcasted_iota(jnp.int32, (rows, PAGE), 0) // N_HEADS
    s_new = jnp.where(lax.broadcasted_iota(jnp.int32, (rows, PAGE), 1) <= t_of_row, s_new, NEG)

    m = jnp.max(s_new, axis=1, keepdims=True)
    for s in s_pages:
        m = jnp.maximum(m, jnp.max(s, axis=1, keepdims=True))
    p_new = jnp.exp(s_new - m)
    l = jnp.sum(p_new, axis=1, keepdims=True)
    o = _dot(p_new, vn)
    for pg in range(n_pages):
        p = jnp.exp(s_pages[pg] - m)
        l = l + jnp.sum(p, axis=1, keepdims=True)
        o = o + _dot_nt(p, v_refs[pg][0, 0])
    o = o / l
    for t in range(n_t):
        blk = jnp.where(head_mask, o[t * N_HEADS:(t + 1) * N_HEADS, :], 0.0)
        o_ref[0, t:t + 1, :] = jnp.sum(blk, axis=0, keepdims=True)


def _attn_sample(pt_flat, q, kn, vn, lfn, ckt, cvt, clft, *, layer, n_pages):
    n_seq, n_t, _ = q.shape

    def page_spec(block, pg):
        nz = len(block) - 2
        return pl.BlockSpec(block, lambda s, pt: (layer, pt[pg * n_seq + s]) + (0,) * nz)

    per_seq = lambda r, width: pl.BlockSpec((1, r, width), lambda s, pt: (s, 0, 0))
    in_specs = ([per_seq(n_t, ATT_W), per_seq(n_t, ATT_W), per_seq(n_t, ATT_W), per_seq(N_HEADS, PAGE)]
                + [page_spec((1, 1, ATT_W, PAGE), pg) for pg in range(n_pages)]
                + [page_spec((1, 1, ATT_W, PAGE), pg) for pg in range(n_pages)]
                + [page_spec((1, 1, N_HEADS, PAGE), pg) for pg in range(n_pages)])
    return pl.pallas_call(
        functools.partial(_attn_sample_kernel, n_pages=n_pages, n_t=n_t),
        out_shape=jax.ShapeDtypeStruct((n_seq, n_t, ATT_W), F32),
        grid_spec=pltpu.PrefetchScalarGridSpec(
            num_scalar_prefetch=1,
            grid=(n_seq,),
            in_specs=in_specs,
            out_specs=pl.BlockSpec((1, n_t, ATT_W), lambda s, pt: (s, 0, 0)),
        ),
        compiler_params=pltpu.CompilerParams(dimension_semantics=("arbitrary",), vmem_limit_bytes=VMEM_LIMIT),
        name="attn_sample",
    )(pt_flat, q, kn, vn, lfn, *([ckt] * n_pages), *([cvt] * n_pages), *([clft] * n_pages))


def _layer_weights(l, w_in, b_forget, conv_w, pool_w, pool_scale, w_br_attn, w_br_conv, w_br_pool, w_o,
                   ln1_g, ln1_b, w_gate_up, w_down, ln2_g, ln2_b):
    wi = w_in[l]
    row = lambda v: v[l].reshape(1, -1)
    return dict(
        wqkv=wi[:, OFF_Q:OFF_F].astype(BF16),
        wf=jnp.pad(wi[:, OFF_F:OFF_CB], ((0, 0), (0, LANES - N_HEADS))).astype(BF16),
        bf=jnp.pad(row(b_forget), ((0, 0), (0, LANES - N_HEADS))),
        wc=wi[:, OFF_CB:OFF_P].astype(BF16),
        cw=conv_w[l],
        wp=wi[:, OFF_P:OFF_G].astype(BF16),
        pw=pool_w[l].astype(BF16),
        ps=row(pool_scale),
        wg=wi[:, OFF_G:].astype(BF16),
        wbc=w_br_conv[l].astype(BF16),
        wbp=w_br_pool[l].astype(BF16),
        wba=w_br_attn[l].astype(BF16),
        wo=w_o[l].astype(BF16),
        l1g=row(ln1_g), l1b=row(ln1_b),
        wgu=w_gate_up[l].astype(BF16),
        wd=w_down[l].astype(BF16),
        l2g=row(ln2_g), l2b=row(ln2_b),
    )


def kernel(x_prompt, x_sample, cache_k, cache_v, cache_logf, state_conv, state_pool, page_table, c_prompt, c_sample, ada_w, ada_b, w_in, b_forget, conv_w, pool_w, pool_scale, w_br_attn, w_br_conv, w_br_pool, w_o, ln1_g, ln1_b, w_gate_up, w_down, ln2_g, ln2_b):
    depth = ada_w.shape[0]
    alpha = float((2 * depth) ** 0.25)
    nb, seq, d = x_prompt.shape
    n_seq, n_t, _ = x_sample.shape
    n_pool = cache_k.shape[1]
    n_pages = page_table.shape[1]
    past_len = n_pages * PAGE
    tm = 256

    mod = _modulation(jnp.concatenate([c_prompt, c_sample], axis=0), ada_w, ada_b)
    mod_p = mod[:, :nb].reshape(depth, nb, 1, 6 * d)
    mod_s = mod[:, nb:]

    ckt = cache_k.transpose(0, 1, 3, 4, 2).reshape(depth, n_pool, ATT_W, PAGE)
    cvt = cache_v.transpose(0, 1, 3, 4, 2).reshape(depth, n_pool, ATT_W, PAGE)
    clft = cache_logf.transpose(0, 1, 3, 2)
    pt_flat = page_table.T.reshape(-1)

    xp = x_prompt
    xs = x_sample.transpose(1, 0, 2).reshape(n_t * n_seq, d)
    cprev_t = state_conv.transpose(0, 2, 1, 3)
    pprev_t = state_pool.transpose(0, 2, 1, 3)

    outs_p = [[] for _ in range(5)]
    outs_s = [[] for _ in range(5)]
    for l in range(depth):
        w = _layer_weights(l, w_in, b_forget, conv_w, pool_w, pool_scale, w_br_attn, w_br_conv, w_br_pool, w_o,
                           ln1_g, ln1_b, w_gate_up, w_down, ln2_g, ln2_b)
        q, kt, vt, lft, dcol, drow, g0, pm, cst, pst = _front_prompt(xp, mod_p[l], w, tm=tm)
        att = _attn_prompt(q, kt, vt, dcol, drow, tq=tm)
        xp = _back_prompt(xp, att, g0, pm, mod_p[l], w, tm=tm, alpha=alpha)
        outs_p[0].append(kt.reshape(nb, N_HEADS, HEAD_DIM, seq).transpose(0, 3, 1, 2))
        outs_p[1].append(vt.reshape(nb, N_HEADS, HEAD_DIM, seq).transpose(0, 3, 1, 2))
        outs_p[2].append(lft.transpose(0, 2, 1))
        outs_p[3].append(cst[:, CONV_HIST - (CONV_K - 1):])
        outs_p[4].append(pst[:, POOL_HIST - POOL_STATE:])

        qs, ks, vs, kts, vts, lfts, g0s, pms, us, ps = _front_sample(
            xs, mod_s[l], cprev_t[l], pprev_t[l], w, n_t=n_t, past_len=past_len)
        seq_major = lambda a: a.reshape(n_t, n_seq, -1).transpose(1, 0, 2)
        lfn = jnp.pad(lfts.transpose(2, 1, 0), ((0, 0), (0, 0), (0, PAGE - n_t)))
        att_s = _attn_sample(pt_flat, seq_major(qs), seq_major(ks), seq_major(vs), lfn, ckt, cvt, clft,
                             layer=l, n_pages=n_pages)
        att_t = att_s.transpose(1, 0, 2).reshape(n_t * n_seq, ATT_W)
        xs = _back_sample(xs, att_t, g0s, pms, mod_s[l], w, steps_per_tile=2, alpha=alpha)
        outs_s[0].append(kts.reshape(n_t, N_HEADS, HEAD_DIM, n_seq).transpose(3, 0, 1, 2))
        outs_s[1].append(vts.reshape(n_t, N_HEADS, HEAD_DIM, n_seq).transpose(3, 0, 1, 2))
        outs_s[2].append(lfts.transpose(2, 0, 1))
        outs_s[3].append(seq_major(us)[:, n_t - (CONV_K - 1):])
        outs_s[4].append(jnp.concatenate([state_pool[l], seq_major(ps)], axis=1)[:, -POOL_STATE:])

    y_sample = xs.reshape(n_t, n_seq, d).transpose(1, 0, 2)
    return (xp, y_sample) + tuple(jnp.stack(a) for a in outs_p) + tuple(jnp.stack(a) for a in outs_s)
```

```python
import functools

import jax
import jax.numpy as jnp
from jax import lax
from jax.experimental import pallas as pl
from jax.experimental.pallas import tpu as pltpu

F32 = jnp.float32
BF16 = jnp.bfloat16

D_MODEL = 1024
N_HEADS = 8
HEAD_DIM = 64
ATT_W = N_HEADS * HEAD_DIM
CONV_W = 512
CONV_K = 3
POOL_WINDOWS = (2, 4, 8, 16)
POOL_GW = 128
POOL_W = len(POOL_WINDOWS) * POOL_GW
POOL_STATE = max(POOL_WINDOWS) - 1
POOL_HIST = POOL_STATE + 1
CONV_HIST = 8
FF = 2816
FF_CHUNK = 1408
LN_EPS = 1e-5
PAGE = 128
LANES = 128
NEG = -1e30
QK_AHEAD = 3

OFF_Q = 0
OFF_F = 3 * ATT_W
OFF_CB = OFF_F + N_HEADS
OFF_P = OFF_CB + 3 * CONV_W
OFF_G = OFF_P + POOL_W

VMEM_LIMIT = 58 * 1024 * 1024


def _dot(a, b):
    return jnp.dot(a, b, preferred_element_type=F32)


def _dot_nt(a, b):
    return lax.dot_general(a, b, (((1,), (1,)), ((), ())), preferred_element_type=F32)


def _dot_hi(a, b):
    return jnp.dot(a, b, preferred_element_type=F32, precision=lax.Precision.HIGHEST)


def _log_sigmoid(x):
    return jnp.minimum(x, 0.0) - jnp.log1p(jnp.exp(-jnp.abs(x)))


def _silu(x):
    return x * jax.nn.sigmoid(x)


def _layer_norm(x, g, b):
    mu = jnp.mean(x, axis=-1, keepdims=True)
    xc = x - mu
    var = jnp.mean(xc * xc, axis=-1, keepdims=True)
    return xc * lax.rsqrt(var + LN_EPS) * g + b


def _const_spec(shape):
    nd = len(shape)
    return pl.BlockSpec(shape, lambda *_: (0,) * nd, pipeline_mode=pl.Buffered(1))


def _mod_kernel(c_ref, w_ref, b_ref, o_ref):
    c = c_ref[...]
    o_ref[0] = _dot(_silu(c).astype(BF16), w_ref[0].astype(BF16)) + b_ref[0]


def _modulation(c_all, ada_w, ada_b):
    depth, d, n = ada_w.shape
    rows = c_all.shape[0]
    tn = 1536
    return pl.pallas_call(
        _mod_kernel,
        out_shape=jax.ShapeDtypeStruct((depth, rows, n), F32),
        grid=(depth, n // tn),
        in_specs=[pl.BlockSpec((rows, d), lambda l, j: (0, 0)),
                  pl.BlockSpec((1, d, tn), lambda l, j: (l, 0, j)),
                  pl.BlockSpec((1, 1, tn), lambda l, j: (l, 0, j))],
        out_specs=pl.BlockSpec((1, rows, tn), lambda l, j: (l, 0, j)),
        compiler_params=pltpu.CompilerParams(vmem_limit_bytes=VMEM_LIMIT),
        name="modulation",
    )(c_all, ada_w, ada_b.reshape(depth, 1, n))


def _gated_merge_partial(h, conv_out, pool_out, wg_ref, wbc_ref, wbp_ref):
    d = D_MODEL
    g_att = jax.nn.sigmoid(_dot(h, wg_ref[:, 0:d]))
    g_conv = jax.nn.sigmoid(_dot(h, wg_ref[:, d:2 * d]))
    g_pool = jax.nn.sigmoid(_dot(h, wg_ref[:, 2 * d:3 * d]))
    pm = g_conv * _dot(conv_out, wbc_ref[...]) + g_pool * _dot(pool_out, wbp_ref[...])
    return g_att, pm


def _front_prompt_kernel(x_ref, mod_ref, wqkv_ref, wf_ref, bf_ref, wc_ref, cw_ref, wp_ref, pw_ref, ps_ref,
                         wg_ref, wbc_ref, wbp_ref,
                         qt_ref, kb_ref, kt_ref, vt_ref, lft_ref, dcol_ref, drow_ref, g0_ref, pm_ref, cst_ref, pst_ref,
                         ucar, pcar, dcar, *, tm):
    d = D_MODEL
    ti = pl.program_id(1)

    @pl.when(ti == 0)
    def _():
        ucar[...] = jnp.zeros_like(ucar)
        pcar[...] = jnp.zeros_like(pcar)
        dcar[...] = jnp.zeros_like(dcar)

    x = x_ref[0]
    sh = mod_ref[0, :, 0:d]
    sc = mod_ref[0, :, d:2 * d]
    h = (x * (1.0 + sc) + sh).astype(BF16)

    qkv = _dot(h, wqkv_ref[...])
    qt_ref[0] = (qkv[:, 0:ATT_W] * (HEAD_DIM ** -0.5)).T.astype(BF16)
    kb_ref[0] = qkv[:, ATT_W:2 * ATT_W].astype(BF16)
    kt_ref[0] = qkv[:, ATT_W:2 * ATT_W].T
    vt_ref[0] = qkv[:, 2 * ATT_W:3 * ATT_W].T

    lf = _log_sigmoid(_dot(h, wf_ref[...]) + bf_ref[...])
    r = lax.broadcasted_iota(jnp.int32, (tm, tm), 0)
    c = lax.broadcasted_iota(jnp.int32, (tm, tm), 1)
    tri = jnp.where(c <= r, 1.0, 0.0).astype(F32)
    dcum = _dot_hi(tri, lf) + dcar[...]
    dcar[...] = dcum[tm - 1:tm, :]
    lft_ref[0] = lf.T[0:N_HEADS, :]
    dcol_ref[0] = dcum[:, 0:N_HEADS]
    drow_ref[0] = dcum.T[0:N_HEADS, :]

    zc = _dot(h, wc_ref[...])
    gb = zc[:, 0:CONV_W]
    u = zc[:, CONV_W:2 * CONV_W] * zc[:, 2 * CONV_W:3 * CONV_W]
    uext = jnp.concatenate([ucar[...], u], axis=0)
    u1 = pltpu.roll(uext, 1, axis=0)[CONV_HIST:, :]
    u2 = pltpu.roll(uext, 2, axis=0)[CONV_HIST:, :]
    y = cw_ref[0:1, :] * u2 + cw_ref[1:2, :] * u1 + cw_ref[2:3, :] * u
    conv_out = (gb * y).astype(BF16)
    ucar[...] = u[tm - CONV_HIST:, :]
    cst_ref[0] = u[tm - CONV_HIST:, :]

    p = _dot(h, wp_ref[...])
    pext = jnp.concatenate([pcar[...], p], axis=0)
    pos1 = ti * tm + lax.broadcasted_iota(jnp.int32, (tm, POOL_GW), 0) + 1
    pouts = []
    for g, w in enumerate(POOL_WINDOWS):
        cols = slice(g * POOL_GW, (g + 1) * POOL_GW)
        s = pext[:, cols]
        k = 1
        while k < w:
            s = s + pltpu.roll(s, k, axis=0)
            k *= 2
        cnt = jnp.minimum(pos1, w).astype(F32)
        pooled = (s[POOL_HIST:, :] / cnt - p[:, cols]).astype(BF16)
        pouts.append(_dot(pooled, pw_ref[g]))
    pool_out = (jnp.concatenate(pouts, axis=1) * ps_ref[...]).astype(BF16)
    pcar[...] = p[tm - POOL_HIST:, :]
    pst_ref[0] = p[tm - POOL_HIST:, :]

    g_att, pm = _gated_merge_partial(h, conv_out, pool_out, wg_ref, wbc_ref, wbp_ref)
    g0_ref[0] = g_att
    pm_ref[0] = pm


def _front_prompt(x, mod, w, *, tm):
    b, t, d = x.shape
    nt = t // tm
    tile = lambda width: pl.BlockSpec((1, tm, width), lambda i, j: (i, j, 0))
    tile_t = lambda rows: pl.BlockSpec((1, rows, tm), lambda i, j: (i, 0, j))
    per_b = lambda rows, width: pl.BlockSpec((1, rows, width), lambda i, j: (i, 0, 0))
    weights = (w["wqkv"], w["wf"], w["bf"], w["wc"], w["cw"], w["wp"], w["pw"], w["ps"], w["wg"], w["wbc"], w["wbp"])
    return pl.pallas_call(
        functools.partial(_front_prompt_kernel, tm=tm),
        out_shape=(jax.ShapeDtypeStruct((b, ATT_W, t), BF16),
                   jax.ShapeDtypeStruct((b, t, ATT_W), BF16),
                   jax.ShapeDtypeStruct((b, ATT_W, t), F32),
                   jax.ShapeDtypeStruct((b, ATT_W, t), F32),
                   jax.ShapeDtypeStruct((b, N_HEADS, t), F32),
                   jax.ShapeDtypeStruct((b, t, N_HEADS), F32),
                   jax.ShapeDtypeStruct((b, N_HEADS, t), F32),
                   jax.ShapeDtypeStruct((b, t, d), F32),
                   jax.ShapeDtypeStruct((b, t, d), F32),
                   jax.ShapeDtypeStruct((b, CONV_HIST, CONV_W), F32),
                   jax.ShapeDtypeStruct((b, POOL_HIST, POOL_W), F32)),
        grid=(b, nt),
        in_specs=[tile(d), pl.BlockSpec((1, 1, 6 * d), lambda i, j: (i, 0, 0))] + [_const_spec(a.shape) for a in weights],
        out_specs=(tile_t(ATT_W), tile(ATT_W), tile_t(ATT_W), tile_t(ATT_W), tile_t(N_HEADS), tile(N_HEADS),
                   tile_t(N_HEADS), tile(d), tile(d), per_b(CONV_HIST, CONV_W), per_b(POOL_HIST, POOL_W)),
        scratch_shapes=[pltpu.VMEM((CONV_HIST, CONV_W), F32), pltpu.VMEM((POOL_HIST, POOL_W), F32),
                        pltpu.VMEM((1, LANES), F32)],
        compiler_params=pltpu.CompilerParams(dimension_semantics=("parallel", "arbitrary"),
                                             vmem_limit_bytes=VMEM_LIMIT),
        name="front_prompt",
    )(x, mod, *weights)


def _attn_prompt_kernel(qt_ref, kb_ref, vt_ref, drow_ref, dcol_ref, o_ref, w_scr, acc_scr, *, tq):
    qi = pl.program_id(1)
    q0 = pl.multiple_of(qi * tq, tq)
    zeros = jnp.zeros((HEAD_DIM, tq), BF16)
    for hd in range(N_HEADS):
        qh = qt_ref[0, hd * HEAD_DIM:(hd + 1) * HEAD_DIM, :]
        w_scr[hd] = jnp.concatenate([qh, zeros] if hd % 2 == 0 else [zeros, qh], axis=0)
    dq = drow_ref[0, :, pl.ds(q0, tq)]
    causal = lax.broadcasted_iota(jnp.int32, (tq, tq), 0) <= lax.broadcasted_iota(jnp.int32, (tq, tq), 1)

    def block(k0, stats):
        def logits(hd):
            hp = hd // 2
            k2 = kb_ref[0, pl.ds(k0, tq), hp * LANES:(hp + 1) * LANES]
            one = slice(hd, hd + 1)
            return _dot(k2, w_scr[hd]) + dq[one, :] - dcol_ref[0, pl.ds(k0, tq), one]

        pending = {hd: logits(hd) for hd in range(QK_AHEAD)}
        new_stats = []
        for hd in range(N_HEADS):
            if hd + QK_AHEAD < N_HEADS:
                pending[hd + QK_AHEAD] = logits(hd + QK_AHEAD)
            s = pending.pop(hd)
            rows = slice(hd * HEAD_DIM, (hd + 1) * HEAD_DIM)
            vt = vt_ref[0, rows, pl.ds(k0, tq)].astype(BF16)
            if stats is None:
                s = jnp.where(causal, s, NEG)
                m_new = jnp.max(s, axis=0, keepdims=True)
                p = jnp.exp(s - m_new)
                l_new = jnp.sum(p, axis=0, keepdims=True)
                acc_scr[rows, :] = _dot(vt, p.astype(BF16))
            else:
                m_old, l_old = stats[hd]
                m_new = jnp.maximum(m_old, jnp.max(s, axis=0, keepdims=True))
                a = jnp.exp(m_old - m_new)
                p = jnp.exp(s - m_new)
                l_new = a * l_old + jnp.sum(p, axis=0, keepdims=True)
                acc_scr[rows, :] = a * acc_scr[rows, :] + _dot(vt, p.astype(BF16))
            new_stats.append((m_new, l_new))
        return tuple(new_stats)

    stats = lax.fori_loop(0, qi, lambda j, st: block(pl.multiple_of(j * tq, tq), st), block(q0, None))
    for hd in range(N_HEADS):
        rows = slice(hd * HEAD_DIM, (hd + 1) * HEAD_DIM)
        o_ref[0, rows, :] = acc_scr[rows, :] / stats[hd][1]


def _attn_prompt(qt, kb, vt, drow, dcol, *, tq):
    b, _, t = qt.shape
    per_b = lambda rows, width: pl.BlockSpec((1, rows, width), lambda i, j: (i, 0, 0))
    return pl.pallas_call(
        functools.partial(_attn_prompt_kernel, tq=tq),
        out_shape=jax.ShapeDtypeStruct((b, ATT_W, t), F32),
        grid=(b, t // tq),
        in_specs=[pl.BlockSpec((1, ATT_W, tq), lambda i, j: (i, 0, j)),
                  per_b(t, ATT_W), per_b(ATT_W, t), per_b(N_HEADS, t), per_b(t, N_HEADS)],
        out_specs=pl.BlockSpec((1, ATT_W, tq), lambda i, j: (i, 0, j)),
        scratch_shapes=[pltpu.VMEM((N_HEADS, 2 * HEAD_DIM, tq), BF16), pltpu.VMEM((ATT_W, tq), F32)],
        compiler_params=pltpu.CompilerParams(dimension_semantics=("parallel", "arbitrary"),
                                             vmem_limit_bytes=VMEM_LIMIT),
        name="attn_prompt",
    )(qt, kb, vt, drow, dcol)


def _back_kernel(x_ref, att_ref, g0_ref, pm_ref, mod_ref, wba_ref, wo_ref, l1g_ref, l1b_ref, wgu_ref, wd_ref,
                 l2g_ref, l2b_ref, o_ref, *, alpha, mod_rep, att_transposed):
    d = D_MODEL

    def mod(k):
        if mod_rep == 0:
            return mod_ref[0, :, k * d:(k + 1) * d]
        m = mod_ref[:, k * d:(k + 1) * d]
        return jnp.concatenate([m] * mod_rep, axis=0)

    x = x_ref[...].reshape(x_ref.shape[-2:])
    att = att_ref[...].reshape(att_ref.shape[-2:])
    if att_transposed:
        att = att.T
    g0 = g0_ref[...].reshape(x.shape)
    pm = pm_ref[...].reshape(x.shape)
    merged = g0 * _dot(att.astype(BF16), wba_ref[...]) + pm
    mix = _dot(merged.astype(BF16), wo_ref[...])
    x1 = _layer_norm(alpha * x + mod(2) * mix, l1g_ref[...], l1b_ref[...])
    h2 = (x1 * (1.0 + mod(4)) + mod(3)).astype(BF16)
    f = None
    for c0 in range(0, FF, FF_CHUNK):
        a = _dot(h2, wgu_ref[:, c0:c0 + FF_CHUNK])
        b = _dot(h2, wgu_ref[:, FF + c0:FF + c0 + FF_CHUNK])
        fc = _dot((_silu(a) * b).astype(BF16), wd_ref[c0:c0 + FF_CHUNK, :])
        f = fc if f is None else f + fc
    out = _layer_norm(alpha * x1 + mod(5) * f, l2g_ref[...], l2b_ref[...])
    o_ref[...] = out.reshape(o_ref.shape)


def _back_weights(w):
    return (w["wba"], w["wo"], w["l1g"], w["l1b"], w["wgu"], w["wd"], w["l2g"], w["l2b"])


def _back_prompt(x, att, g0, pm, mod, w, *, tm, alpha):
    b, t, d = x.shape
    tile = lambda width: pl.BlockSpec((1, tm, width), lambda i, j: (i, j, 0))
    weights = _back_weights(w)
    return pl.pallas_call(
        functools.partial(_back_kernel, alpha=alpha, mod_rep=0, att_transposed=True),
        out_shape=jax.ShapeDtypeStruct((b, t, d), F32),
        grid=(b, t // tm),
        in_specs=[tile(d), pl.BlockSpec((1, ATT_W, tm), lambda i, j: (i, 0, j)), tile(d), tile(d),
                  pl.BlockSpec((1, 1, 6 * d), lambda i, j: (i, 0, 0))]
                 + [_const_spec(a.shape) for a in weights],
        out_specs=tile(d),
        compiler_params=pltpu.CompilerParams(dimension_semantics=("parallel", "parallel"),
                                             vmem_limit_bytes=VMEM_LIMIT),
        name="back_prompt",
    )(x, att, g0, pm, mod, *weights)


def _back_sample(x, att, g0, pm, mod, w, *, steps_per_tile, alpha):
    rows, d = x.shape
    n_seq = mod.shape[0]
    tm = steps_per_tile * n_seq
    tile = lambda width: pl.BlockSpec((tm, width), lambda i: (i, 0))
    weights = _back_weights(w)
    return pl.pallas_call(
        functools.partial(_back_kernel, alpha=alpha, mod_rep=steps_per_tile, att_transposed=False),
        out_shape=jax.ShapeDtypeStruct((rows, d), F32),
        grid=(rows // tm,),
        in_specs=[tile(d), tile(ATT_W), tile(d), tile(d), _const_spec(mod.shape)]
                 + [_const_spec(a.shape) for a in weights],
        out_specs=tile(d),
        compiler_params=pltpu.CompilerParams(dimension_semantics=("parallel",), vmem_limit_bytes=VMEM_LIMIT),
        name="back_sample",
    )(x, att, g0, pm, mod, *weights)


def _front_sample_kernel(x_ref, mod_ref, cprev_ref, pprev_ref, wqkv_ref, wf_ref, bf_ref, wc_ref, cw_ref, wp_ref,
                         pw_ref, ps_ref, wg_ref, wbc_ref, wbp_ref,
                         q_ref, k_ref, v_ref, kt_ref, vt_ref, lft_ref, g0_ref, pm_ref, u_ref, p_ref,
                         ucar, phist, *, past_len):
    d = D_MODEL
    t = pl.program_id(0)

    @pl.when(t == 0)
    def _():
        ucar[...] = cprev_ref[...]
        phist[...] = pprev_ref[...]

    sh = mod_ref[:, 0:d]
    sc = mod_ref[:, d:2 * d]
    h = (x_ref[...] * (1.0 + sc) + sh).astype(BF16)

    qkv = _dot(h, wqkv_ref[...])
    q_ref[...] = qkv[:, 0:ATT_W]
    k_ref[...] = qkv[:, ATT_W:2 * ATT_W]
    v_ref[...] = qkv[:, 2 * ATT_W:3 * ATT_W]
    kt_ref[0] = qkv[:, ATT_W:2 * ATT_W].T
    vt_ref[0] = qkv[:, 2 * ATT_W:3 * ATT_W].T

    lf = _log_sigmoid(_dot(h, wf_ref[...]) + bf_ref[...])
    lft_ref[0] = lf.T[0:N_HEADS, :]

    zc = _dot(h, wc_ref[...])
    gb = zc[:, 0:CONV_W]
    u = zc[:, CONV_W:2 * CONV_W] * zc[:, 2 * CONV_W:3 * CONV_W]
    y = cw_ref[0:1, :] * ucar[0] + cw_ref[1:2, :] * ucar[1] + cw_ref[2:3, :] * u
    conv_out = (gb * y).astype(BF16)
    ucar[0] = ucar[1]
    ucar[1] = u
    u_ref[...] = u

    p = _dot(h, wp_ref[...])
    p_ref[...] = p
    pouts = []
    for g, w in enumerate(POOL_WINDOWS):
        cols = slice(g * POOL_GW, (g + 1) * POOL_GW)
        win = p[:, cols]
        for j in range(1, w):
            win = win + phist[POOL_STATE - j, :, cols]
        cnt = jnp.minimum(past_len + t + 1, w).astype(F32)
        pooled = (win / cnt - p[:, cols]).astype(BF16)
        pouts.append(_dot(pooled, pw_ref[g]))
    pool_out = (jnp.concatenate(pouts, axis=1) * ps_ref[...]).astype(BF16)
    for k in range(POOL_STATE - 1):
        phist[k] = phist[k + 1]
    phist[POOL_STATE - 1] = p

    g_att, pm = _gated_merge_partial(h, conv_out, pool_out, wg_ref, wbc_ref, wbp_ref)
    g0_ref[...] = g_att
    pm_ref[...] = pm


def _front_sample(x, mod, cprev, pprev, w, *, n_t, past_len):
    rows, d = x.shape
    n_seq = rows // n_t
    tile = lambda width: pl.BlockSpec((n_seq, width), lambda i: (i, 0))
    tile_t = lambda r: pl.BlockSpec((1, r, n_seq), lambda i: (i, 0, 0))
    weights = (w["wqkv"], w["wf"], w["bf"], w["wc"], w["cw"], w["wp"], w["pw"], w["ps"], w["wg"], w["wbc"], w["wbp"])
    return pl.pallas_call(
        functools.partial(_front_sample_kernel, past_len=past_len),
        out_shape=(jax.ShapeDtypeStruct((rows, ATT_W), F32),
                   jax.ShapeDtypeStruct((rows, ATT_W), F32),
                   jax.ShapeDtypeStruct((rows, ATT_W), F32),
                   jax.ShapeDtypeStruct((n_t, ATT_W, n_seq), F32),
                   jax.ShapeDtypeStruct((n_t, ATT_W, n_seq), F32),
                   jax.ShapeDtypeStruct((n_t, N_HEADS, n_seq), F32),
                   jax.ShapeDtypeStruct((rows, d), F32),
                   jax.ShapeDtypeStruct((rows, d), F32),
                   jax.ShapeDtypeStruct((rows, CONV_W), F32),
                   jax.ShapeDtypeStruct((rows, POOL_W), F32)),
        grid=(n_t,),
        in_specs=[tile(d), _const_spec(mod.shape), _const_spec(cprev.shape), _const_spec(pprev.shape)]
                 + [_const_spec(a.shape) for a in weights],
        out_specs=(tile(ATT_W), tile(ATT_W), tile(ATT_W), tile_t(ATT_W), tile_t(ATT_W), tile_t(N_HEADS),
                   tile(d), tile(d), tile(CONV_W), tile(POOL_W)),
        scratch_shapes=[pltpu.VMEM((CONV_K - 1, n_seq, CONV_W), F32),
                        pltpu.VMEM((POOL_STATE, n_seq, POOL_W), F32)],
        compiler_params=pltpu.CompilerParams(dimension_semantics=("arbitrary",), vmem_limit_bytes=VMEM_LIMIT),
        name="front_sample",
    )(x, mod, cprev, pprev, *weights)


def _attn_sample_kernel(pt_ref, q_ref, kn_ref, vn_ref, lfn_ref, *rest, n_pages, n_t):
    del pt_ref
    k_refs = rest[0:n_pages]
    v_refs = rest[n_pages:2 * n_pages]
    lf_refs = rest[2 * n_pages:3 * n_pages]
    o_ref = rest[3 * n_pages]
    rows = n_t * N_HEADS

    q = q_ref[0] * (HEAD_DIM ** -0.5)
    head_of_lane = lax.broadcasted_iota(jnp.int32, (N_HEADS, ATT_W), 1) // HEAD_DIM
    head_mask = head_of_lane == lax.broadcasted_iota(jnp.int32, (N_HEADS, ATT_W), 0)
    qbd = jnp.concatenate(
        [jnp.where(head_mask, jnp.broadcast_to(q[t:t + 1, :], (N_HEADS, ATT_W)), 0.0) for t in range(n_t)], axis=0)

    li = lax.broadcasted_iota(jnp.int32, (PAGE, PAGE), 0)
    lj = lax.broadcasted_iota(jnp.int32, (PAGE, PAGE), 1)
    cn = _dot_hi(lfn_ref[0], jnp.where(li <= lj, 1.0, 0.0).astype(F32))
    lf_all = jnp.concatenate([lf_refs[pg][0, 0] for pg in range(n_pages)], axis=0)
    r_in = _dot_hi(lf_all, jnp.where(li > lj, 1.0, 0.0).astype(F32))
    tot = jnp.sum(lf_all, axis=1, keepdims=True)
    carry = jnp.zeros((N_HEADS, 1), F32)
    r_pages = [None] * n_pages
    for pg in reversed(range(n_pages)):
        sl = slice(pg * N_HEADS, (pg + 1) * N_HEADS)
        r_pages[pg] = r_in[sl, :] + carry
        carry = carry + tot[sl, :]
    cn_col = jnp.concatenate([cn[:, t:t + 1] for t in range(n_t)], axis=0)

    s_pages = [_dot(qbd, k_refs[pg][0, 0]) + jnp.concatenate([r_pages[pg]] * n_t, axis=0) + cn_col
               for pg in range(n_pages)]
    pad = jnp.zeros((PAGE - n_t, ATT_W), F32)
    kn = jnp.concatenate([kn_ref[0], pad], axis=0)
    vn = jnp.concatenate([vn_ref[0], pad], axis=0)
    s_new = _dot_nt(qbd, kn) + cn_col - jnp.concatenate([cn] * n_t, axis=0)
    t_of_row = lax.broadcasted_iota(jnp.int32, (rows, PAGE), 0) // N_HEADS
    s_new = jnp.where(lax.broadcasted_iota(jnp.int32, (rows, PAGE), 1) <= t_of_row, s_new, NEG)

    m = jnp.max(s_new, axis=1, keepdims=True)
    for s in s_pages:
        m = jnp.maximum(m, jnp.max(s, axis=1, keepdims=True))
    p_new = jnp.exp(s_new - m)
    l = jnp.sum(p_new, axis=1, keepdims=True)
    o = _dot(p_new, vn)
    for pg in range(n_pages):
        p = jnp.exp(s_pages[pg] - m)
        l = l + jnp.sum(p, axis=1, keepdims=True)
        o = o + _dot_nt(p, v_refs[pg][0, 0])
    o = o / l
    for t in range(n_t):
        blk = jnp.where(head_mask, o[t * N_HEADS:(t + 1) * N_HEADS, :], 0.0)
        o_ref[0, t:t + 1, :] = jnp.sum(blk, axis=0, keepdims=True)


def _attn_sample(pt_flat, q, kn, vn, lfn, ckt, cvt, clft, *, layer, n_pages):
    n_seq, n_t, _ = q.shape

    def page_spec(block, pg):
        nz = len(block) - 2
        return pl.BlockSpec(block, lambda s, pt: (layer, pt[pg * n_seq + s]) + (0,) * nz)

    per_seq = lambda r, width: pl.BlockSpec((1, r, width), lambda s, pt: (s, 0, 0))
    in_specs = ([per_seq(n_t, ATT_W), per_seq(n_t, ATT_W), per_seq(n_t, ATT_W), per_seq(N_HEADS, PAGE)]
                + [page_spec((1, 1, ATT_W, PAGE), pg) for pg in range(n_pages)]
                + [page_spec((1, 1, ATT_W, PAGE), pg) for pg in range(n_pages)]
                + [page_spec((1, 1, N_HEADS, PAGE), pg) for pg in range(n_pages)])
    return pl.pallas_call(
        functools.partial(_attn_sample_kernel, n_pages=n_pages, n_t=n_t),
        out_shape=jax.ShapeDtypeStruct((n_seq, n_t, ATT_W), F32),
        grid_spec=pltpu.PrefetchScalarGridSpec(
            num_scalar_prefetch=1,
            grid=(n_seq,),
            in_specs=in_specs,
            out_specs=pl.BlockSpec((1, n_t, ATT_W), lambda s, pt: (s, 0, 0)),
        ),
        compiler_params=pltpu.CompilerParams(dimension_semantics=("arbitrary",), vmem_limit_bytes=VMEM_LIMIT),
        name="attn_sample",
    )(pt_flat, q, kn, vn, lfn, *([ckt] * n_pages), *([cvt] * n_pages), *([clft] * n_pages))


def _layer_weights(l, w_in, b_forget, conv_w, pool_w, pool_scale, w_br_attn, w_br_conv, w_br_pool, w_o,
                   ln1_g, ln1_b, w_gate_up, w_down, ln2_g, ln2_b):
    wi = w_in[l]
    row = lambda v: v[l].reshape(1, -1)
    return dict(
        wqkv=wi[:, OFF_Q:OFF_F].astype(BF16),
        wf=jnp.pad(wi[:, OFF_F:OFF_CB], ((0, 0), (0, LANES - N_HEADS))).astype(BF16),
        bf=jnp.pad(row(b_forget), ((0, 0), (0, LANES - N_HEADS))),
        wc=wi[:, OFF_CB:OFF_P].astype(BF16),
        cw=conv_w[l],
        wp=wi[:, OFF_P:OFF_G].astype(BF16),
        pw=pool_w[l].astype(BF16),
        ps=row(pool_scale),
        wg=wi[:, OFF_G:].astype(BF16),
        wbc=w_br_conv[l].astype(BF16),
        wbp=w_br_pool[l].astype(BF16),
        wba=w_br_attn[l].astype(BF16),
        wo=w_o[l].astype(BF16),
        l1g=row(ln1_g), l1b=row(ln1_b),
        wgu=w_gate_up[l].astype(BF16),
        wd=w_down[l].astype(BF16),
        l2g=row(ln2_g), l2b=row(ln2_b),
    )


def kernel(x_prompt, x_sample, cache_k, cache_v, cache_logf, state_conv, state_pool, page_table, c_prompt, c_sample, ada_w, ada_b, w_in, b_forget, conv_w, pool_w, pool_scale, w_br_attn, w_br_conv, w_br_pool, w_o, ln1_g, ln1_b, w_gate_up, w_down, ln2_g, ln2_b):
    depth = ada_w.shape[0]
    alpha = float((2 * depth) ** 0.25)
    nb, seq, d = x_prompt.shape
    n_seq, n_t, _ = x_sample.shape
    n_pool = cache_k.shape[1]
    n_pages = page_table.shape[1]
    past_len = n_pages * PAGE
    tm = 256

    mod = _modulation(jnp.concatenate([c_prompt, c_sample], axis=0), ada_w, ada_b)
    mod_p = mod[:, :nb].reshape(depth, nb, 1, 6 * d)
    mod_s = mod[:, nb:]

    ckt = cache_k.transpose(0, 1, 3, 4, 2).reshape(depth, n_pool, ATT_W, PAGE)
    cvt = cache_v.transpose(0, 1, 3, 4, 2).reshape(depth, n_pool, ATT_W, PAGE)
    clft = cache_logf.transpose(0, 1, 3, 2)
    pt_flat = page_table.T.reshape(-1)

    xp = x_prompt
    xs = x_sample.transpose(1, 0, 2).reshape(n_t * n_seq, d)
    cprev_t = state_conv.transpose(0, 2, 1, 3)
    pprev_t = state_pool.transpose(0, 2, 1, 3)

    outs_p = [[] for _ in range(5)]
    outs_s = [[] for _ in range(5)]
    for l in range(depth):
        w = _layer_weights(l, w_in, b_forget, conv_w, pool_w, pool_scale, w_br_attn, w_br_conv, w_br_pool, w_o,
                           ln1_g, ln1_b, w_gate_up, w_down, ln2_g, ln2_b)
        qt, kb, kt, vt, lft, dcol, drow, g0, pm, cst, pst = _front_prompt(xp, mod_p[l], w, tm=tm)
        att = _attn_prompt(qt, kb, vt, drow, dcol, tq=tm)
        xp = _back_prompt(xp, att, g0, pm, mod_p[l], w, tm=tm, alpha=alpha)
        outs_p[0].append(kt.reshape(nb, N_HEADS, HEAD_DIM, seq).transpose(0, 3, 1, 2))
        outs_p[1].append(vt.reshape(nb, N_HEADS, HEAD_DIM, seq).transpose(0, 3, 1, 2))
        outs_p[2].append(lft.transpose(0, 2, 1))
        outs_p[3].append(cst[:, CONV_HIST - (CONV_K - 1):])
        outs_p[4].append(pst[:, POOL_HIST - POOL_STATE:])

        qs, ks, vs, kts, vts, lfts, g0s, pms, us, ps = _front_sample(
            xs, mod_s[l], cprev_t[l], pprev_t[l], w, n_t=n_t, past_len=past_len)
        seq_major = lambda a: a.reshape(n_t, n_seq, -1).transpose(1, 0, 2)
        lfn = jnp.pad(lfts.transpose(2, 1, 0), ((0, 0), (0, 0), (0, PAGE - n_t)))
        att_s = _attn_sample(pt_flat, seq_major(qs), seq_major(ks), seq_major(vs), lfn, ckt, cvt, clft,
                             layer=l, n_pages=n_pages)
        att_t = att_s.transpose(1, 0, 2).reshape(n_t * n_seq, ATT_W)
        xs = _back_sample(xs, att_t, g0s, pms, mod_s[l], w, steps_per_tile=2, alpha=alpha)
        outs_s[0].append(kts.reshape(n_t, N_HEADS, HEAD_DIM, n_seq).transpose(3, 0, 1, 2))
        outs_s[1].append(vts.reshape(n_t, N_HEADS, HEAD_DIM, n_seq).transpose(3, 0, 1, 2))
        outs_s[2].append(lfts.transpose(2, 0, 1))
        outs_s[3].append(seq_major(us)[:, n_t - (CONV_K - 1):])
        outs_s[4].append(jnp.concatenate([state_pool[l], seq_major(ps)], axis=1)[:, -POOL_STATE:])

    y_sample = xs.reshape(n_t, n_seq, d).transpose(1, 0, 2)
    return (xp, y_sample) + tuple(jnp.stack(a) for a in outs_p) + tuple(jnp.stack(a) for a in outs_s)
```

```python
import functools

import jax
import jax.numpy as jnp
from jax import lax
from jax.experimental import pallas as pl
from jax.experimental.pallas import tpu as pltpu

F32 = jnp.float32
BF16 = jnp.bfloat16

D_MODEL = 1024
N_HEADS = 8
HEAD_DIM = 64
ATT_W = N_HEADS * HEAD_DIM
CONV_W = 512
CONV_K = 3
POOL_WINDOWS = (2, 4, 8, 16)
POOL_GW = 128
POOL_W = len(POOL_WINDOWS) * POOL_GW
POOL_STATE = max(POOL_WINDOWS) - 1
POOL_HIST = POOL_STATE + 1
CONV_HIST = 8
FF = 2816
FF_CHUNK = 1408
LN_EPS = 1e-5
PAGE = 128
LANES = 128
NEG = -1e30
QK_AHEAD = 8
LOG2E = 1.4426950408889634

OFF_Q = 0
OFF_F = 3 * ATT_W
OFF_CB = OFF_F + N_HEADS
OFF_P = OFF_CB + 3 * CONV_W
OFF_G = OFF_P + POOL_W

VMEM_LIMIT = 58 * 1024 * 1024


def _dot(a, b):
    return jnp.dot(a, b, preferred_element_type=F32)


def _dot_nt(a, b):
    return lax.dot_general(a, b, (((1,), (1,)), ((), ())), preferred_element_type=F32)


def _dot_hi(a, b):
    return jnp.dot(a, b, preferred_element_type=F32, precision=lax.Precision.HIGHEST)


def _log_sigmoid(x):
    return jnp.minimum(x, 0.0) - jnp.log1p(jnp.exp(-jnp.abs(x)))


def _silu(x):
    return x * jax.nn.sigmoid(x)


def _layer_norm(x, g, b):
    mu = jnp.mean(x, axis=-1, keepdims=True)
    xc = x - mu
    var = jnp.mean(xc * xc, axis=-1, keepdims=True)
    return xc * lax.rsqrt(var + LN_EPS) * g + b


def _const_spec(shape):
    nd = len(shape)
    return pl.BlockSpec(shape, lambda *_: (0,) * nd, pipeline_mode=pl.Buffered(1))


def _mod_kernel(c_ref, w_ref, b_ref, o_ref):
    c = c_ref[...]
    o_ref[0] = _dot(_silu(c).astype(BF16), w_ref[0].astype(BF16)) + b_ref[0]


def _modulation(c_all, ada_w, ada_b):
    depth, d, n = ada_w.shape
    rows = c_all.shape[0]
    tn = 1536
    return pl.pallas_call(
        _mod_kernel,
        out_shape=jax.ShapeDtypeStruct((depth, rows, n), F32),
        grid=(depth, n // tn),
        in_specs=[pl.BlockSpec((rows, d), lambda l, j: (0, 0)),
                  pl.BlockSpec((1, d, tn), lambda l, j: (l, 0, j)),
                  pl.BlockSpec((1, 1, tn), lambda l, j: (l, 0, j))],
        out_specs=pl.BlockSpec((1, rows, tn), lambda l, j: (l, 0, j)),
        compiler_params=pltpu.CompilerParams(vmem_limit_bytes=VMEM_LIMIT),
        name="modulation",
    )(c_all, ada_w, ada_b.reshape(depth, 1, n))


def _gated_merge_partial(h, conv_out, pool_out, wg_ref, wbc_ref, wbp_ref):
    d = D_MODEL
    g_att = jax.nn.sigmoid(_dot(h, wg_ref[:, 0:d]))
    g_conv = jax.nn.sigmoid(_dot(h, wg_ref[:, d:2 * d]))
    g_pool = jax.nn.sigmoid(_dot(h, wg_ref[:, 2 * d:3 * d]))
    pm = g_conv * _dot(conv_out, wbc_ref[...]) + g_pool * _dot(pool_out, wbp_ref[...])
    return g_att, pm


def _split3(x):
    hi = x.astype(BF16).astype(F32)
    r = x - hi
    mid = r.astype(BF16).astype(F32)
    lo = (r - mid).astype(BF16).astype(F32)
    return hi, mid, lo


def _front_prompt_kernel(*refs, tm, n_alias):
    (x_ref, mod_ref, wqkv_ref, wf_ref, bf_ref, wc_ref, cw_ref, wp_ref, pw_ref, ps_ref,
     wg_ref, wbc_ref, wbp_ref) = refs[:13]
    (qt_ref, kb_ref, kaux_ref, kt_ref, vt_ref, lft_ref, drow_ref, g0_ref, pm_ref, cst_ref, pst_ref,
     ucar, pcar, dcar) = refs[13 + n_alias:]
    d = D_MODEL
    ti = pl.program_id(1)

    @pl.when(ti == 0)
    def _():
        ucar[...] = jnp.zeros_like(ucar)
        pcar[...] = jnp.zeros_like(pcar)
        dcar[...] = jnp.zeros_like(dcar)

    x = x_ref[0]
    sh = mod_ref[0, :, 0:d]
    sc = mod_ref[0, :, d:2 * d]
    h = (x * (1.0 + sc) + sh).astype(BF16)

    qkv = _dot(h, wqkv_ref[...])
    qt_ref[0] = (qkv[:, 0:ATT_W] * (HEAD_DIM ** -0.5 * LOG2E)).T.astype(BF16)
    kb_ref[0] = qkv[:, ATT_W:2 * ATT_W].astype(BF16)
    kt_ref[0, 0] = qkv[:, ATT_W:2 * ATT_W].T
    vt_ref[0, 0] = qkv[:, 2 * ATT_W:3 * ATT_W].T

    lf = _log_sigmoid(_dot(h, wf_ref[...]) + bf_ref[...])
    r = lax.broadcasted_iota(jnp.int32, (tm, tm), 0)
    c = lax.broadcasted_iota(jnp.int32, (tm, tm), 1)
    tri = jnp.where(c <= r, 1.0, 0.0).astype(F32)
    dcum = _dot_hi(tri, lf) + dcar[...]
    dcar[...] = dcum[tm - 1:tm, :]
    lft_ref[0] = lf.T[0:N_HEADS, :]
    d2 = dcum * LOG2E
    drow_ref[0] = d2.T[0:N_HEADS, :]
    hi, mid, lo = _split3(-d2)
    lane = lax.broadcasted_iota(jnp.int32, (tm, LANES), 1)
    aux = jnp.where(lane < N_HEADS, hi,
                    jnp.where(lane < 2 * N_HEADS, pltpu.roll(mid, N_HEADS, axis=1),
                              jnp.where(lane < 3 * N_HEADS, pltpu.roll(lo, 2 * N_HEADS, axis=1),
                                        jnp.where(lane < 3 * N_HEADS + 3, 1.0, 0.0))))
    kaux_ref[0] = aux.astype(BF16)

    zc = _dot(h, wc_ref[...])
    gb = zc[:, 0:CONV_W]
    u = zc[:, CONV_W:2 * CONV_W] * zc[:, 2 * CONV_W:3 * CONV_W]
    uext = jnp.concatenate([ucar[...], u], axis=0)
    u1 = pltpu.roll(uext, 1, axis=0)[CONV_HIST:, :]
    u2 = pltpu.roll(uext, 2, axis=0)[CONV_HIST:, :]
    y = cw_ref[0:1, :] * u2 + cw_ref[1:2, :] * u1 + cw_ref[2:3, :] * u
    conv_out = (gb * y).astype(BF16)
    ucar[...] = u[tm - CONV_HIST:, :]
    cst_ref[0] = u[tm - CONV_HIST:, :]

    p = _dot(h, wp_ref[...])
    pext = jnp.concatenate([pcar[...], p], axis=0)
    pos1 = ti * tm + lax.broadcasted_iota(jnp.int32, (tm, POOL_GW), 0) + 1
    pouts = []
    for g, w in enumerate(POOL_WINDOWS):
        cols = slice(g * POOL_GW, (g + 1) * POOL_GW)
        s = pext[:, cols]
        k = 1
        while k < w:
            s = s + pltpu.roll(s, k, axis=0)
            k *= 2
        cnt = jnp.minimum(pos1, w).astype(F32)
        pooled = (s[POOL_HIST:, :] / cnt - p[:, cols]).astype(BF16)
        pouts.append(_dot(pooled, pw_ref[g]))
    pool_out = (jnp.concatenate(pouts, axis=1) * ps_ref[...]).astype(BF16)
    pcar[...] = p[tm - POOL_HIST:, :]
    pst_ref[0] = p[tm - POOL_HIST:, :]

    g_att, pm = _gated_merge_partial(h, conv_out, pool_out, wg_ref, wbc_ref, wbp_ref)
    g0_ref[0] = g_att
    pm_ref[0] = pm


def _front_prompt(x, mod, w, kv_all, *, tm, layer, depth):
    b, t, d = x.shape
    nt = t // tm
    tile = lambda width: pl.BlockSpec((1, tm, width), lambda i, j: (i, j, 0))
    tile_t = lambda rows: pl.BlockSpec((1, rows, tm), lambda i, j: (i, 0, j))
    per_b = lambda rows, width: pl.BlockSpec((1, rows, width), lambda i, j: (i, 0, 0))
    slab_t = pl.BlockSpec((1, 1, ATT_W, tm), lambda i, j: (layer, i, 0, j))
    weights = (w["wqkv"], w["wf"], w["bf"], w["wc"], w["cw"], w["wp"], w["pw"], w["ps"], w["wg"], w["wbc"], w["wbp"])
    aliased = () if kv_all is None else tuple(kv_all)
    n_in = 2 + len(weights)
    return pl.pallas_call(
        functools.partial(_front_prompt_kernel, tm=tm, n_alias=len(aliased)),
        out_shape=(jax.ShapeDtypeStruct((b, ATT_W, t), BF16),
                   jax.ShapeDtypeStruct((b, t, ATT_W), BF16),
                   jax.ShapeDtypeStruct((b, t, LANES), BF16),
                   jax.ShapeDtypeStruct((depth, b, ATT_W, t), F32),
                   jax.ShapeDtypeStruct((depth, b, ATT_W, t), F32),
                   jax.ShapeDtypeStruct((b, N_HEADS, t), F32),
                   jax.ShapeDtypeStruct((b, N_HEADS, t), F32),
                   jax.ShapeDtypeStruct((b, t, d), F32),
                   jax.ShapeDtypeStruct((b, t, d), F32),
                   jax.ShapeDtypeStruct((b, CONV_HIST, CONV_W), F32),
                   jax.ShapeDtypeStruct((b, POOL_HIST, POOL_W), F32)),
        grid=(b, nt),
        in_specs=[tile(d), pl.BlockSpec((1, 1, 6 * d), lambda i, j: (i, 0, 0))] + [_const_spec(a.shape) for a in weights]
                 + [pl.BlockSpec(memory_space=pl.ANY)] * len(aliased),
        out_specs=(tile_t(ATT_W), tile(ATT_W), tile(LANES), slab_t, slab_t, tile_t(N_HEADS), tile_t(N_HEADS),
                   tile(d), tile(d), per_b(CONV_HIST, CONV_W), per_b(POOL_HIST, POOL_W)),
        scratch_shapes=[pltpu.VMEM((CONV_HIST, CONV_W), F32), pltpu.VMEM((POOL_HIST, POOL_W), F32),
                        pltpu.VMEM((1, LANES), F32)],
        input_output_aliases={n_in + k: 3 + k for k in range(len(aliased))},
        compiler_params=pltpu.CompilerParams(dimension_semantics=("parallel", "arbitrary"),
                                             vmem_limit_bytes=VMEM_LIMIT),
        name="front_prompt",
    )(x, mod, *weights, *aliased)


def _attn_prompt_kernel(qt_ref, kb_ref, kaux_ref, vt_ref, drow_ref, o_ref, w_scr, acc_scr, *, tq):
    qi = pl.program_id(1)
    q0 = pl.multiple_of(qi * tq, tq)
    zeros = jnp.zeros((HEAD_DIM, tq), BF16)
    brow = lax.broadcasted_iota(jnp.int32, (LANES, tq), 0)
    dq = drow_ref[0, :, pl.ds(q0, tq)]
    for hd in range(N_HEADS):
        qh = qt_ref[0, hd * HEAD_DIM:(hd + 1) * HEAD_DIM, :]
        top = jnp.concatenate([qh, zeros] if hd % 2 == 0 else [zeros, qh], axis=0)
        hi, mid, lo = _split3(dq[hd:hd + 1, :])
        bias = jnp.where(brow == 3 * N_HEADS, hi, jnp.where(brow == 3 * N_HEADS + 1, mid,
                         jnp.where(brow == 3 * N_HEADS + 2, lo, 0.0)))
        for part in range(3):
            bias = jnp.where(brow == part * N_HEADS + hd, 1.0, bias)
        w_scr[hd] = jnp.concatenate([top, bias.astype(BF16)], axis=0)
    causal = lax.broadcasted_iota(jnp.int32, (tq, tq), 0) <= lax.broadcasted_iota(jnp.int32, (tq, tq), 1)

    def block(k0, stats):
        kaux = kaux_ref[0, pl.ds(k0, tq), :]

        def logits(hd):
            hp = hd // 2
            k2 = kb_ref[0, pl.ds(k0, tq), hp * LANES:(hp + 1) * LANES]
            return _dot(jnp.concatenate([k2, kaux], axis=1), w_scr[hd])

        pending = {hd: logits(hd) for hd in range(QK_AHEAD)}
        new_stats = []
        for hd in range(N_HEADS):
            if hd + QK_AHEAD < N_HEADS:
                pending[hd + QK_AHEAD] = logits(hd + QK_AHEAD)
            s = pending.pop(hd)
            rows = slice(hd * HEAD_DIM, (hd + 1) * HEAD_DIM)
            vt = vt_ref[0, 0, rows, pl.ds(k0, tq)].astype(BF16)
            if stats is None:
                s = jnp.where(causal, s, NEG)
                m_new = jnp.max(s, axis=0, keepdims=True)
                p = jnp.exp2(s - m_new)
                l_new = jnp.sum(p, axis=0, keepdims=True)
                acc_scr[rows, :] = _dot(vt, p.astype(BF16))
            else:
                m_old, l_old = stats[hd]
                m_new = jnp.maximum(m_old, jnp.max(s, axis=0, keepdims=True))
                a = jnp.exp2(m_old - m_new)
                p = jnp.exp2(s - m_new)
                l_new = a * l_old + jnp.sum(p, axis=0, keepdims=True)
                acc_scr[rows, :] = a * acc_scr[rows, :] + _dot(vt, p.astype(BF16))
            new_stats.append((m_new, l_new))
        return tuple(new_stats)

    stats = lax.fori_loop(0, qi, lambda j, st: block(pl.multiple_of(j * tq, tq), st), block(q0, None))
    for hd in range(N_HEADS):
        rows = slice(hd * HEAD_DIM, (hd + 1) * HEAD_DIM)
        o_ref[0, rows, :] = acc_scr[rows, :] / stats[hd][1]


def _attn_prompt(qt, kb, kaux, vt_all, drow, *, tq, layer):
    b, _, t = qt.shape
    per_b = lambda rows, width: pl.BlockSpec((1, rows, width), lambda i, j: (i, 0, 0))
    return pl.pallas_call(
        functools.partial(_attn_prompt_kernel, tq=tq),
        out_shape=jax.ShapeDtypeStruct((b, ATT_W, t), F32),
        grid=(b, t // tq),
        in_specs=[pl.BlockSpec((1, ATT_W, tq), lambda i, j: (i, 0, j)),
                  per_b(t, ATT_W), per_b(t, LANES),
                  pl.BlockSpec((1, 1, ATT_W, t), lambda i, j: (layer, i, 0, 0)),
                  per_b(N_HEADS, t)],
        out_specs=pl.BlockSpec((1, ATT_W, tq), lambda i, j: (i, 0, j)),
        scratch_shapes=[pltpu.VMEM((N_HEADS, 2 * LANES, tq), BF16), pltpu.VMEM((ATT_W, tq), F32)],
        compiler_params=pltpu.CompilerParams(dimension_semantics=("parallel", "arbitrary"),
                                             vmem_limit_bytes=VMEM_LIMIT),
        name="attn_prompt",
    )(qt, kb, kaux, vt_all, drow)


def _back_kernel(x_ref, att_ref, g0_ref, pm_ref, mod_ref, wba_ref, wo_ref, l1g_ref, l1b_ref, wgu_ref, wd_ref,
                 l2g_ref, l2b_ref, o_ref, *, alpha, mod_rep, att_transposed):
    d = D_MODEL

    def mod(k):
        if mod_rep == 0:
            return mod_ref[0, :, k * d:(k + 1) * d]
        m = mod_ref[:, k * d:(k + 1) * d]
        return jnp.concatenate([m] * mod_rep, axis=0)

    x = x_ref[...].reshape(x_ref.shape[-2:])
    att = att_ref[...].reshape(att_ref.shape[-2:])
    if att_transposed:
        att = att.T
    g0 = g0_ref[...].reshape(x.shape)
    pm = pm_ref[...].reshape(x.shape)
    merged = g0 * _dot(att.astype(BF16), wba_ref[...]) + pm
    mix = _dot(merged.astype(BF16), wo_ref[...])
    x1 = _layer_norm(alpha * x + mod(2) * mix, l1g_ref[...], l1b_ref[...])
    h2 = (x1 * (1.0 + mod(4)) + mod(3)).astype(BF16)
    f = None
    for c0 in range(0, FF, FF_CHUNK):
        a = _dot(h2, wgu_ref[:, c0:c0 + FF_CHUNK])
        b = _dot(h2, wgu_ref[:, FF + c0:FF + c0 + FF_CHUNK])
        fc = _dot((_silu(a) * b).astype(BF16), wd_ref[c0:c0 + FF_CHUNK, :])
        f = fc if f is None else f + fc
    out = _layer_norm(alpha * x1 + mod(5) * f, l2g_ref[...], l2b_ref[...])
    o_ref[...] = out.reshape(o_ref.shape)


def _back_weights(w):
    return (w["wba"], w["wo"], w["l1g"], w["l1b"], w["wgu"], w["wd"], w["l2g"], w["l2b"])


def _back_prompt(x, att, g0, pm, mod, w, *, tm, alpha):
    b, t, d = x.shape
    tile = lambda width: pl.BlockSpec((1, tm, width), lambda i, j: (i, j, 0))
    weights = _back_weights(w)
    return pl.pallas_call(
        functools.partial(_back_kernel, alpha=alpha, mod_rep=0, att_transposed=True),
        out_shape=jax.ShapeDtypeStruct((b, t, d), F32),
        grid=(b, t // tm),
        in_specs=[tile(d), pl.BlockSpec((1, ATT_W, tm), lambda i, j: (i, 0, j)), tile(d), tile(d),
                  pl.BlockSpec((1, 1, 6 * d), lambda i, j: (i, 0, 0))]
                 + [_const_spec(a.shape) for a in weights],
        out_specs=tile(d),
        compiler_params=pltpu.CompilerParams(dimension_semantics=("parallel", "parallel"),
                                             vmem_limit_bytes=VMEM_LIMIT),
        name="back_prompt",
    )(x, att, g0, pm, mod, *weights)


def _back_sample(x, att, g0, pm, mod, w, *, steps_per_tile, alpha):
    rows, d = x.shape
    n_seq = mod.shape[0]
    tm = steps_per_tile * n_seq
    tile = lambda width: pl.BlockSpec((tm, width), lambda i: (i, 0))
    weights = _back_weights(w)
    return pl.pallas_call(
        functools.partial(_back_kernel, alpha=alpha, mod_rep=steps_per_tile, att_transposed=False),
        out_shape=jax.ShapeDtypeStruct((rows, d), F32),
        grid=(rows // tm,),
        in_specs=[tile(d), tile(ATT_W), tile(d), tile(d), _const_spec(mod.shape)]
                 + [_const_spec(a.shape) for a in weights],
        out_specs=tile(d),
        compiler_params=pltpu.CompilerParams(dimension_semantics=("parallel",), vmem_limit_bytes=VMEM_LIMIT),
        name="back_sample",
    )(x, att, g0, pm, mod, *weights)


def _front_sample_kernel(x_ref, mod_ref, cprev_ref, pprev_ref, wqkv_ref, wf_ref, bf_ref, wc_ref, cw_ref, wp_ref,
                         pw_ref, ps_ref, wg_ref, wbc_ref, wbp_ref,
                         q_ref, k_ref, v_ref, kt_ref, vt_ref, lft_ref, g0_ref, pm_ref, u_ref, p_ref,
                         ucar, phist, *, past_len):
    d = D_MODEL
    t = pl.program_id(0)

    @pl.when(t == 0)
    def _():
        ucar[...] = cprev_ref[...]
        phist[...] = pprev_ref[...]

    sh = mod_ref[:, 0:d]
    sc = mod_ref[:, d:2 * d]
    h = (x_ref[...] * (1.0 + sc) + sh).astype(BF16)

    qkv = _dot(h, wqkv_ref[...])
    q_ref[...] = qkv[:, 0:ATT_W]
    k_ref[...] = qkv[:, ATT_W:2 * ATT_W]
    v_ref[...] = qkv[:, 2 * ATT_W:3 * ATT_W]
    kt_ref[0] = qkv[:, ATT_W:2 * ATT_W].T
    vt_ref[0] = qkv[:, 2 * ATT_W:3 * ATT_W].T

    lf = _log_sigmoid(_dot(h, wf_ref[...]) + bf_ref[...])
    lft_ref[0] = lf.T[0:N_HEADS, :]

    zc = _dot(h, wc_ref[...])
    gb = zc[:, 0:CONV_W]
    u = zc[:, CONV_W:2 * CONV_W] * zc[:, 2 * CONV_W:3 * CONV_W]
    y = cw_ref[0:1, :] * ucar[0] + cw_ref[1:2, :] * ucar[1] + cw_ref[2:3, :] * u
    conv_out = (gb * y).astype(BF16)
    ucar[0] = ucar[1]
    ucar[1] = u
    u_ref[...] = u

    p = _dot(h, wp_ref[...])
    p_ref[...] = p
    pouts = []
    for g, w in enumerate(POOL_WINDOWS):
        cols = slice(g * POOL_GW, (g + 1) * POOL_GW)
        win = p[:, cols]
        for j in range(1, w):
            win = win + phist[POOL_STATE - j, :, cols]
        cnt = jnp.minimum(past_len + t + 1, w).astype(F32)
        pooled = (win / cnt - p[:, cols]).astype(BF16)
        pouts.append(_dot(pooled, pw_ref[g]))
    pool_out = (jnp.concatenate(pouts, axis=1) * ps_ref[...]).astype(BF16)
    for k in range(POOL_STATE - 1):
        phist[k] = phist[k + 1]
    phist[POOL_STATE - 1] = p

    g_att, pm = _gated_merge_partial(h, conv_out, pool_out, wg_ref, wbc_ref, wbp_ref)
    g0_ref[...] = g_att
    pm_ref[...] = pm


def _front_sample(x, mod, cprev, pprev, w, *, n_t, past_len):
    rows, d = x.shape
    n_seq = rows // n_t
    tile = lambda width: pl.BlockSpec((n_seq, width), lambda i: (i, 0))
    tile_t = lambda r: pl.BlockSpec((1, r, n_seq), lambda i: (i, 0, 0))
    weights = (w["wqkv"], w["wf"], w["bf"], w["wc"], w["cw"], w["wp"], w["pw"], w["ps"], w["wg"], w["wbc"], w["wbp"])
    return pl.pallas_call(
        functools.partial(_front_sample_kernel, past_len=past_len),
        out_shape=(jax.ShapeDtypeStruct((rows, ATT_W), F32),
                   jax.ShapeDtypeStruct((rows, ATT_W), F32),
                   jax.ShapeDtypeStruct((rows, ATT_W), F32),
                   jax.ShapeDtypeStruct((n_t, ATT_W, n_seq), F32),
                   jax.ShapeDtypeStruct((n_t, ATT_W, n_seq), F32),
                   jax.ShapeDtypeStruct((n_t, N_HEADS, n_seq), F32),
                   jax.ShapeDtypeStruct((rows, d), F32),
                   jax.ShapeDtypeStruct((rows, d), F32),
                   jax.ShapeDtypeStruct((rows, CONV_W), F32),
                   jax.ShapeDtypeStruct((rows, POOL_W), F32)),
        grid=(n_t,),
        in_specs=[tile(d), _const_spec(mod.shape), _const_spec(cprev.shape), _const_spec(pprev.shape)]
                 + [_const_spec(a.shape) for a in weights],
        out_specs=(tile(ATT_W), tile(ATT_W), tile(ATT_W), tile_t(ATT_W), tile_t(ATT_W), tile_t(N_HEADS),
                   tile(d), tile(d), tile(CONV_W), tile(POOL_W)),
        scratch_shapes=[pltpu.VMEM((CONV_K - 1, n_seq, CONV_W), F32),
                        pltpu.VMEM((POOL_STATE, n_seq, POOL_W), F32)],
        compiler_params=pltpu.CompilerParams(dimension_semantics=("arbitrary",), vmem_limit_bytes=VMEM_LIMIT),
        name="front_sample",
    )(x, mod, cprev, pprev, *weights)


def _attn_sample_kernel(pt_ref, q_ref, kn_ref, vn_ref, lfn_ref, *rest, n_pages, n_t):
    del pt_ref
    k_refs = rest[0:n_pages]
    v_refs = rest[n_pages:2 * n_pages]
    lf_refs = rest[2 * n_pages:3 * n_pages]
    o_ref = rest[3 * n_pages]
    rows = n_t * N_HEADS

    q = q_ref[0] * (HEAD_DIM ** -0.5)
    head_of_lane = lax.broadcasted_iota(jnp.int32, (N_HEADS, ATT_W), 1) // HEAD_DIM
    head_mask = head_of_lane == lax.broadcasted_iota(jnp.int32, (N_HEADS, ATT_W), 0)
    qbd = jnp.concatenate(
        [jnp.where(head_mask, jnp.broadcast_to(q[t:t + 1, :], (N_HEADS, ATT_W)), 0.0) for t in range(n_t)], axis=0)

    li = lax.broadcasted_iota(jnp.int32, (PAGE, PAGE), 0)
    lj = lax.broadcasted_iota(jnp.int32, (PAGE, PAGE), 1)
    cn = _dot_hi(lfn_ref[0], jnp.where(li <= lj, 1.0, 0.0).astype(F32))
    lf_all = jnp.concatenate([lf_refs[pg][0, 0] for pg in range(n_pages)], axis=0)
    r_in = _dot_hi(lf_all, jnp.where(li > lj, 1.0, 0.0).astype(F32))
    tot = jnp.sum(lf_all, axis=1, keepdims=True)
    carry = jnp.zeros((N_HEADS, 1), F32)
    r_pages = [None] * n_pages
    for pg in reversed(range(n_pages)):
        sl = slice(pg * N_HEADS, (pg + 1) * N_HEADS)
        r_pages[pg] = r_in[sl, :] + carry
        carry = carry + tot[sl, :]
    cn_col = jnp.concatenate([cn[:, t:t + 1] for t in range(n_t)], axis=0)

    s_pages = [_dot(qbd, k_refs[pg][0, 0]) + jnp.concatenate([r_pages[pg]] * n_t, axis=0) + cn_col
               for pg in range(n_pages)]
    pad = jnp.zeros((PAGE - n_t, ATT_W), F32)
    kn = jnp.concatenate([kn_ref[0], pad], axis=0)
    vn = jnp.concatenate([vn_ref[0], pad], axis=0)
    s_new = _dot_nt(qbd, kn) + cn_col - jnp.concatenate([cn] * n_t, axis=0)
    t_of_row = lax.broadcasted_iota(jnp.int32, (rows, PAGE), 0) // N_HEADS
    s_new = jnp.where(lax.broadcasted_iota(jnp.int32, (rows, PAGE), 1) <= t_of_row, s_new, NEG)

    m = jnp.max(s_new, axis=1, keepdims=True)
    for s in s_pages:
        m = jnp.maximum(m, jnp.max(s, axis=1, keepdims=True))
    p_new = jnp.exp(s_new - m)
    l = jnp.sum(p_new, axis=1, keepdims=True)
    o = _dot(p_new, vn)
    for pg in range(n_pages):
        p = jnp.exp(s_pages[pg] - m)
        l = l + jnp.sum(p, axis=1, keepdims=True)
        o = o + _dot_nt(p, v_refs[pg][0, 0])
    o = o / l
    for t in range(n_t):
        blk = jnp.where(head_mask, o[t * N_HEADS:(t + 1) * N_HEADS, :], 0.0)
        o_ref[0, t:t + 1, :] = jnp.sum(blk, axis=0, keepdims=True)


def _attn_sample(pt_flat, q, kn, vn, lfn, ckt, cvt, clft, *, layer, n_pages):
    n_seq, n_t, _ = q.shape

    def page_spec(block, pg):
        nz = len(block) - 2
        return pl.BlockSpec(block, lambda s, pt: (layer, pt[pg * n_seq + s]) + (0,) * nz)

    per_seq = lambda r, width: pl.BlockSpec((1, r, width), lambda s, pt: (s, 0, 0))
    in_specs = ([per_seq(n_t, ATT_W), per_seq(n_t, ATT_W), per_seq(n_t, ATT_W), per_seq(N_HEADS, PAGE)]
                + [page_spec((1, 1, ATT_W, PAGE), pg) for pg in range(n_pages)]
                + [page_spec((1, 1, ATT_W, PAGE), pg) for pg in range(n_pages)]
                + [page_spec((1, 1, N_HEADS, PAGE), pg) for pg in range(n_pages)])
    return pl.pallas_call(
        functools.partial(_attn_sample_kernel, n_pages=n_pages, n_t=n_t),
        out_shape=jax.ShapeDtypeStruct((n_seq, n_t, ATT_W), F32),
        grid_spec=pltpu.PrefetchScalarGridSpec(
            num_scalar_prefetch=1,
            grid=(n_seq,),
            in_specs=in_specs,
            out_specs=pl.BlockSpec((1, n_t, ATT_W), lambda s, pt: (s, 0, 0)),
        ),
        compiler_params=pltpu.CompilerParams(dimension_semantics=("arbitrary",), vmem_limit_bytes=VMEM_LIMIT),
        name="attn_sample",
    )(pt_flat, q, kn, vn, lfn, *([ckt] * n_pages), *([cvt] * n_pages), *([clft] * n_pages))


def _layer_weights(l, w_in, b_forget, conv_w, pool_w, pool_scale, w_br_attn, w_br_conv, w_br_pool, w_o,
                   ln1_g, ln1_b, w_gate_up, w_down, ln2_g, ln2_b):
    wi = w_in[l]
    row = lambda v: v[l].reshape(1, -1)
    return dict(
        wqkv=wi[:, OFF_Q:OFF_F].astype(BF16),
        wf=jnp.pad(wi[:, OFF_F:OFF_CB], ((0, 0), (0, LANES - N_HEADS))).astype(BF16),
        bf=jnp.pad(row(b_forget), ((0, 0), (0, LANES - N_HEADS))),
        wc=wi[:, OFF_CB:OFF_P].astype(BF16),
        cw=conv_w[l],
        wp=wi[:, OFF_P:OFF_G].astype(BF16),
        pw=pool_w[l].astype(BF16),
        ps=row(pool_scale),
        wg=wi[:, OFF_G:].astype(BF16),
        wbc=w_br_conv[l].astype(BF16),
        wbp=w_br_pool[l].astype(BF16),
        wba=w_br_attn[l].astype(BF16),
        wo=w_o[l].astype(BF16),
        l1g=row(ln1_g), l1b=row(ln1_b),
        wgu=w_gate_up[l].astype(BF16),
        wd=w_down[l].astype(BF16),
        l2g=row(ln2_g), l2b=row(ln2_b),
    )


def kernel(x_prompt, x_sample, cache_k, cache_v, cache_logf, state_conv, state_pool, page_table, c_prompt, c_sample, ada_w, ada_b, w_in, b_forget, conv_w, pool_w, pool_scale, w_br_attn, w_br_conv, w_br_pool, w_o, ln1_g, ln1_b, w_gate_up, w_down, ln2_g, ln2_b):
    depth = ada_w.shape[0]
    alpha = float((2 * depth) ** 0.25)
    nb, seq, d = x_prompt.shape
    n_seq, n_t, _ = x_sample.shape
    n_pool = cache_k.shape[1]
    n_pages = page_table.shape[1]
    past_len = n_pages * PAGE
    tm = 256

    mod = _modulation(jnp.concatenate([c_prompt, c_sample], axis=0), ada_w, ada_b)
    mod_p = mod[:, :nb].reshape(depth, nb, 1, 6 * d)
    mod_s = mod[:, nb:]

    ckt = cache_k.transpose(0, 1, 3, 4, 2).reshape(depth, n_pool, ATT_W, PAGE)
    cvt = cache_v.transpose(0, 1, 3, 4, 2).reshape(depth, n_pool, ATT_W, PAGE)
    clft = cache_logf.transpose(0, 1, 3, 2)
    pt_flat = page_table.T.reshape(-1)

    xp = x_prompt
    xs = x_sample.transpose(1, 0, 2).reshape(n_t * n_seq, d)
    cprev_t = state_conv.transpose(0, 2, 1, 3)
    pprev_t = state_pool.transpose(0, 2, 1, 3)

    outs_p = [[] for _ in range(3)]
    outs_s = [[] for _ in range(5)]
    kv_all = None
    for l in range(depth):
        w = _layer_weights(l, w_in, b_forget, conv_w, pool_w, pool_scale, w_br_attn, w_br_conv, w_br_pool, w_o,
                           ln1_g, ln1_b, w_gate_up, w_down, ln2_g, ln2_b)
        qt, kb, kaux, kt_all, vt_all, lft, drow, g0, pm, cst, pst = _front_prompt(
            xp, mod_p[l], w, kv_all, tm=tm, layer=l, depth=depth)
        kv_all = (kt_all, vt_all)
        att = _attn_prompt(qt, kb, kaux, vt_all, drow, tq=tm, layer=l)
        xp = _back_prompt(xp, att, g0, pm, mod_p[l], w, tm=tm, alpha=alpha)
        outs_p[0].append(lft.transpose(0, 2, 1))
        outs_p[1].append(cst[:, CONV_HIST - (CONV_K - 1):])
        outs_p[2].append(pst[:, POOL_HIST - POOL_STATE:])

        qs, ks, vs, kts, vts, lfts, g0s, pms, us, ps = _front_sample(
            xs, mod_s[l], cprev_t[l], pprev_t[l], w, n_t=n_t, past_len=past_len)
        seq_major = lambda a: a.reshape(n_t, n_seq, -1).transpose(1, 0, 2)
        lfn = jnp.pad(lfts.transpose(2, 1, 0), ((0, 0), (0, 0), (0, PAGE - n_t)))
        att_s = _attn_sample(pt_flat, seq_major(qs), seq_major(ks), seq_major(vs), lfn, ckt, cvt, clft,
                             layer=l, n_pages=n_pages)
        att_t = att_s.transpose(1, 0, 2).reshape(n_t * n_seq, ATT_W)
        xs = _back_sample(xs, att_t, g0s, pms, mod_s[l], w, steps_per_tile=2, alpha=alpha)
        outs_s[0].append(kts.reshape(n_t, N_HEADS, HEAD_DIM, n_seq).transpose(3, 0, 1, 2))
        outs_s[1].append(vts.reshape(n_t, N_HEADS, HEAD_DIM, n_seq).transpose(3, 0, 1, 2))
        outs_s[2].append(lfts.transpose(2, 0, 1))
        outs_s[3].append(seq_major(us)[:, n_t - (CONV_K - 1):])
        outs_s[4].append(jnp.concatenate([state_pool[l], seq_major(ps)], axis=1)[:, -POOL_STATE:])

    y_sample = xs.reshape(n_t, n_seq, d).transpose(1, 0, 2)
    kv_p = tuple(a.reshape(depth, nb, N_HEADS, HEAD_DIM, seq).transpose(0, 1, 4, 2, 3) for a in kv_all)
    return (xp, y_sample) + kv_p + tuple(jnp.stack(a) for a in outs_p) + tuple(jnp.stack(a) for a in outs_s)
```

```python
import functools

import jax
import jax.numpy as jnp
from jax import lax
from jax.experimental import pallas as pl
from jax.experimental.pallas import tpu as pltpu

F32 = jnp.float32
BF16 = jnp.bfloat16

D_MODEL = 1024
N_HEADS = 8
HEAD_DIM = 64
ATT_W = N_HEADS * HEAD_DIM
CONV_W = 512
CONV_K = 3
POOL_WINDOWS = (2, 4, 8, 16)
POOL_GW = 128
POOL_W = len(POOL_WINDOWS) * POOL_GW
POOL_STATE = max(POOL_WINDOWS) - 1
POOL_HIST = POOL_STATE + 1
CONV_HIST = 8
FF = 2816
FF_CHUNK = 1408
LN_EPS = 1e-5
PAGE = 128
LANES = 128
NEG = -1e30
QK_AHEAD = 8
LOG2E = 1.4426950408889634

OFF_Q = 0
OFF_F = 3 * ATT_W
OFF_CB = OFF_F + N_HEADS
OFF_P = OFF_CB + 3 * CONV_W
OFF_G = OFF_P + POOL_W

VMEM_LIMIT = 58 * 1024 * 1024


def _dot(a, b):
    return jnp.dot(a, b, preferred_element_type=F32)


def _dot_nt(a, b):
    return lax.dot_general(a, b, (((1,), (1,)), ((), ())), preferred_element_type=F32)


def _dot_hi(a, b):
    return jnp.dot(a, b, preferred_element_type=F32, precision=lax.Precision.HIGHEST)


def _log_sigmoid(x):
    return jnp.minimum(x, 0.0) - jnp.log1p(jnp.exp(-jnp.abs(x)))


def _silu(x):
    return x * jax.nn.sigmoid(x)


def _layer_norm(x, g, b):
    mu = jnp.mean(x, axis=-1, keepdims=True)
    xc = x - mu
    var = jnp.mean(xc * xc, axis=-1, keepdims=True)
    return xc * lax.rsqrt(var + LN_EPS) * g + b


def _const_spec(shape):
    nd = len(shape)
    return pl.BlockSpec(shape, lambda *_: (0,) * nd, pipeline_mode=pl.Buffered(1))


def _mod_kernel(c_ref, w_ref, b_ref, o_ref):
    c = c_ref[...]
    o_ref[0] = _dot(_silu(c).astype(BF16), w_ref[0].astype(BF16)) + b_ref[0]


def _modulation(c_all, ada_w, ada_b):
    depth, d, n = ada_w.shape
    rows = c_all.shape[0]
    tn = 1536
    return pl.pallas_call(
        _mod_kernel,
        out_shape=jax.ShapeDtypeStruct((depth, rows, n), F32),
        grid=(depth, n // tn),
        in_specs=[pl.BlockSpec((rows, d), lambda l, j: (0, 0)),
                  pl.BlockSpec((1, d, tn), lambda l, j: (l, 0, j)),
                  pl.BlockSpec((1, 1, tn), lambda l, j: (l, 0, j))],
        out_specs=pl.BlockSpec((1, rows, tn), lambda l, j: (l, 0, j)),
        compiler_params=pltpu.CompilerParams(vmem_limit_bytes=VMEM_LIMIT),
        name="modulation",
    )(c_all, ada_w, ada_b.reshape(depth, 1, n))


def _gated_merge_partial(h, conv_out, pool_out, wg_ref, wbc_ref, wbp_ref):
    d = D_MODEL
    g_att = jax.nn.sigmoid(_dot(h, wg_ref[:, 0:d]))
    g_conv = jax.nn.sigmoid(_dot(h, wg_ref[:, d:2 * d]))
    g_pool = jax.nn.sigmoid(_dot(h, wg_ref[:, 2 * d:3 * d]))
    pm = g_conv * _dot(conv_out, wbc_ref[...]) + g_pool * _dot(pool_out, wbp_ref[...])
    return g_att, pm


def _split3(x):
    hi = x.astype(BF16).astype(F32)
    r = x - hi
    mid = r.astype(BF16).astype(F32)
    lo = (r - mid).astype(BF16).astype(F32)
    return hi, mid, lo


def _front_prompt_kernel(pt_ref, *refs, tm, layer, n_alias, side_cfg):
    (x_ref, mod_ref, wqkv_ref, wf_ref, bf_ref, wc_ref, cw_ref, wp_ref, pw_ref, ps_ref,
     wg_ref, wbc_ref, wbp_ref) = refs[:13]
    n_side = _SideAttention.N_IN if side_cfg else 0
    side_in = refs[13 + n_alias:13 + n_alias + n_side]
    outs = refs[13 + n_alias + n_side:]
    qt_ref, kb_ref, kaux_ref, kt_ref, vt_ref, lft_ref, drow_ref, g0_ref, pm_ref, cst_ref, pst_ref = outs[:11]
    scratch = outs[11 + (1 if side_cfg else 0):]
    ucar, pcar, dcar = scratch[:3]
    d = D_MODEL
    ti = pl.program_id(1)
    side = None
    if side_cfg:
        side = _SideAttention(pt_ref, side_in, outs[11], scratch[3:], side_cfg,
                              pl.program_id(0) * pl.num_programs(1) + ti)
        side.start()

    @pl.when(ti == 0)
    def _():
        ucar[...] = jnp.zeros_like(ucar)
        pcar[...] = jnp.zeros_like(pcar)
        dcar[...] = jnp.zeros_like(dcar)

    x = x_ref[0]
    sh = mod_ref[0, :, 0:d]
    sc = mod_ref[0, :, d:2 * d]
    h = (x * (1.0 + sc) + sh).astype(BF16)

    qkv = _dot(h, wqkv_ref[...])
    qt_ref[0] = (qkv[:, 0:ATT_W] * (HEAD_DIM ** -0.5 * LOG2E)).T.astype(BF16)
    kb_ref[0] = qkv[:, ATT_W:2 * ATT_W].astype(BF16)
    own = layer if n_alias == 0 else 0
    for slab in range(kt_ref.shape[0]):
        if slab == own:
            kt_ref[slab, 0] = qkv[:, ATT_W:2 * ATT_W].T
            vt_ref[slab, 0] = qkv[:, 2 * ATT_W:3 * ATT_W].T
        else:
            kt_ref[slab, 0] = jnp.zeros(kt_ref.shape[2:], F32)
            vt_ref[slab, 0] = jnp.zeros(vt_ref.shape[2:], F32)

    lf = _log_sigmoid(_dot(h, wf_ref[...]) + bf_ref[...])
    r = lax.broadcasted_iota(jnp.int32, (tm, tm), 0)
    c = lax.broadcasted_iota(jnp.int32, (tm, tm), 1)
    tri = jnp.where(c <= r, 1.0, 0.0).astype(F32)
    dcum = _dot_hi(tri, lf) + dcar[...]
    dcar[...] = dcum[tm - 1:tm, :]
    lft_ref[0] = lf.T[0:N_HEADS, :]
    d2 = dcum * LOG2E
    drow_ref[0] = d2.T[0:N_HEADS, :]
    hi, mid, lo = _split3(-d2)
    lane = lax.broadcasted_iota(jnp.int32, (tm, LANES), 1)
    aux = jnp.where(lane < N_HEADS, hi,
                    jnp.where(lane < 2 * N_HEADS, pltpu.roll(mid, N_HEADS, axis=1),
                              jnp.where(lane < 3 * N_HEADS, pltpu.roll(lo, 2 * N_HEADS, axis=1),
                                        jnp.where(lane < 3 * N_HEADS + 3, 1.0, 0.0))))
    kaux_ref[0] = aux.astype(BF16)

    zc = _dot(h, wc_ref[...])
    gb = zc[:, 0:CONV_W]
    u = zc[:, CONV_W:2 * CONV_W] * zc[:, 2 * CONV_W:3 * CONV_W]
    uext = jnp.concatenate([ucar[...], u], axis=0)
    u1 = pltpu.roll(uext, 1, axis=0)[CONV_HIST:, :]
    u2 = pltpu.roll(uext, 2, axis=0)[CONV_HIST:, :]
    y = cw_ref[0:1, :] * u2 + cw_ref[1:2, :] * u1 + cw_ref[2:3, :] * u
    conv_out = (gb * y).astype(BF16)
    ucar[...] = u[tm - CONV_HIST:, :]
    cst_ref[0] = u[tm - CONV_HIST:, :]

    p = _dot(h, wp_ref[...])
    pext = jnp.concatenate([pcar[...], p], axis=0)
    pos1 = ti * tm + lax.broadcasted_iota(jnp.int32, (tm, POOL_GW), 0) + 1
    pouts = []
    for g, w in enumerate(POOL_WINDOWS):
        cols = slice(g * POOL_GW, (g + 1) * POOL_GW)
        s = pext[:, cols]
        k = 1
        while k < w:
            s = s + pltpu.roll(s, k, axis=0)
            k *= 2
        cnt = jnp.minimum(pos1, w).astype(F32)
        pooled = (s[POOL_HIST:, :] / cnt - p[:, cols]).astype(BF16)
        pouts.append(_dot(pooled, pw_ref[g]))
    pool_out = (jnp.concatenate(pouts, axis=1) * ps_ref[...]).astype(BF16)
    pcar[...] = p[tm - POOL_HIST:, :]
    pst_ref[0] = p[tm - POOL_HIST:, :]

    g_att, pm = _gated_merge_partial(h, conv_out, pool_out, wg_ref, wbc_ref, wbp_ref)
    g0_ref[0] = g_att
    pm_ref[0] = pm

    if side:
        side.finish()


def _front_prompt(x, mod, w, kv_all, pt_flat, side, *, tm, layer, depth):
    b, t, d = x.shape
    nt = t // tm
    tile = lambda width: pl.BlockSpec((1, tm, width), lambda i, j, *_: (i, j, 0))
    tile_t = lambda rows: pl.BlockSpec((1, rows, tm), lambda i, j, *_: (i, 0, j))
    per_b = lambda rows, width: pl.BlockSpec((1, rows, width), lambda i, j, *_: (i, 0, 0))
    if kv_all is None:
        slab_t = pl.BlockSpec((depth, 1, ATT_W, tm), lambda i, j, *_: (0, i, 0, j))
    else:
        slab_t = pl.BlockSpec((1, 1, ATT_W, tm), lambda i, j, *_: (layer, i, 0, j))
    weights = (w["wqkv"], w["wf"], w["bf"], w["wc"], w["cw"], w["wp"], w["pw"], w["ps"], w["wg"], w["wbc"], w["wbp"])
    aliased = () if kv_all is None else tuple(kv_all)
    n_in = 1 + 2 + len(weights)
    side_specs, side_args, side_out_spec, side_out_shape, side_scratch, side_cfg = _side_parts(side)
    return pl.pallas_call(
        functools.partial(_front_prompt_kernel, tm=tm, layer=layer, n_alias=len(aliased), side_cfg=side_cfg),
        out_shape=(jax.ShapeDtypeStruct((b, ATT_W, t), BF16),
                   jax.ShapeDtypeStruct((b, t, ATT_W), BF16),
                   jax.ShapeDtypeStruct((b, t, LANES), BF16),
                   jax.ShapeDtypeStruct((depth, b, ATT_W, t), F32),
                   jax.ShapeDtypeStruct((depth, b, ATT_W, t), F32),
                   jax.ShapeDtypeStruct((b, N_HEADS, t), F32),
                   jax.ShapeDtypeStruct((b, N_HEADS, t), F32),
                   jax.ShapeDtypeStruct((b, t, d), F32),
                   jax.ShapeDtypeStruct((b, t, d), F32),
                   jax.ShapeDtypeStruct((b, CONV_HIST, CONV_W), F32),
                   jax.ShapeDtypeStruct((b, POOL_HIST, POOL_W), F32)) + side_out_shape,
        grid_spec=pltpu.PrefetchScalarGridSpec(
            num_scalar_prefetch=1,
            grid=(b, nt),
            in_specs=[tile(d), pl.BlockSpec((1, 1, 6 * d), lambda i, j, *_: (i, 0, 0))]
                     + [_const_spec(a.shape) for a in weights]
                     + [pl.BlockSpec(memory_space=pl.ANY)] * len(aliased) + side_specs,
            out_specs=(tile_t(ATT_W), tile(ATT_W), tile(LANES), slab_t, slab_t, tile_t(N_HEADS), tile_t(N_HEADS),
                       tile(d), tile(d), per_b(CONV_HIST, CONV_W), per_b(POOL_HIST, POOL_W)) + side_out_spec,
            scratch_shapes=[pltpu.VMEM((CONV_HIST, CONV_W), F32), pltpu.VMEM((POOL_HIST, POOL_W), F32),
                            pltpu.VMEM((1, LANES), F32)] + side_scratch),
        input_output_aliases={n_in + k: 3 + k for k in range(len(aliased))},
        compiler_params=pltpu.CompilerParams(dimension_semantics=("arbitrary", "arbitrary"),
                                             vmem_limit_bytes=VMEM_LIMIT),
        name="front_prompt",
    )(pt_flat, x, mod, *weights, *aliased, *side_args)


def _attn_prompt_kernel(qt_ref, kb_ref, kaux_ref, vt_ref, drow_ref, o_ref, w_scr, acc_scr, *, tq):
    qi = pl.program_id(1)
    q0 = pl.multiple_of(qi * tq, tq)
    zeros = jnp.zeros((HEAD_DIM, tq), BF16)
    brow = lax.broadcasted_iota(jnp.int32, (LANES, tq), 0)
    dq = drow_ref[0, :, pl.ds(q0, tq)]
    for hd in range(N_HEADS):
        qh = qt_ref[0, hd * HEAD_DIM:(hd + 1) * HEAD_DIM, :]
        top = jnp.concatenate([qh, zeros] if hd % 2 == 0 else [zeros, qh], axis=0)
        hi, mid, lo = _split3(dq[hd:hd + 1, :])
        bias = jnp.where(brow == 3 * N_HEADS, hi, jnp.where(brow == 3 * N_HEADS + 1, mid,
                         jnp.where(brow == 3 * N_HEADS + 2, lo, 0.0)))
        for part in range(3):
            bias = jnp.where(brow == part * N_HEADS + hd, 1.0, bias)
        w_scr[hd] = jnp.concatenate([top, bias.astype(BF16)], axis=0)
    causal = lax.broadcasted_iota(jnp.int32, (tq, tq), 0) <= lax.broadcasted_iota(jnp.int32, (tq, tq), 1)

    def block(k0, stats):
        kaux = kaux_ref[0, pl.ds(k0, tq), :]

        def logits(hd):
            hp = hd // 2
            k2 = kb_ref[0, pl.ds(k0, tq), hp * LANES:(hp + 1) * LANES]
            return _dot(jnp.concatenate([k2, kaux], axis=1), w_scr[hd])

        pending = {hd: logits(hd) for hd in range(QK_AHEAD)}
        new_stats = []
        for hd in range(N_HEADS):
            if hd + QK_AHEAD < N_HEADS:
                pending[hd + QK_AHEAD] = logits(hd + QK_AHEAD)
            s = pending.pop(hd)
            rows = slice(hd * HEAD_DIM, (hd + 1) * HEAD_DIM)
            vt = vt_ref[0, 0, rows, pl.ds(k0, tq)].astype(BF16)
            if stats is None:
                s = jnp.where(causal, s, NEG)
                m_new = jnp.max(s, axis=0, keepdims=True)
                p = jnp.exp2(s - m_new)
                l_new = jnp.sum(p, axis=0, keepdims=True)
                acc_scr[rows, :] = _dot(vt, p.astype(BF16))
            else:
                m_old, l_old = stats[hd]
                m_new = jnp.maximum(m_old, jnp.max(s, axis=0, keepdims=True))
                a = jnp.exp2(m_old - m_new)
                p = jnp.exp2(s - m_new)
                l_new = a * l_old + jnp.sum(p, axis=0, keepdims=True)
                acc_scr[rows, :] = a * acc_scr[rows, :] + _dot(vt, p.astype(BF16))
            new_stats.append((m_new, l_new))
        return tuple(new_stats)

    stats = lax.fori_loop(0, qi, lambda j, st: block(pl.multiple_of(j * tq, tq), st), block(q0, None))
    for hd in range(N_HEADS):
        rows = slice(hd * HEAD_DIM, (hd + 1) * HEAD_DIM)
        o_ref[0, rows, :] = acc_scr[rows, :] / stats[hd][1]


def _attn_prompt(qt, kb, kaux, vt_all, drow, *, tq, layer):
    b, _, t = qt.shape
    per_b = lambda rows, width: pl.BlockSpec((1, rows, width), lambda i, j: (i, 0, 0))
    return pl.pallas_call(
        functools.partial(_attn_prompt_kernel, tq=tq),
        out_shape=jax.ShapeDtypeStruct((b, ATT_W, t), F32),
        grid=(b, t // tq),
        in_specs=[pl.BlockSpec((1, ATT_W, tq), lambda i, j: (i, 0, j)),
                  per_b(t, ATT_W), per_b(t, LANES),
                  pl.BlockSpec((1, 1, ATT_W, t), lambda i, j: (layer, i, 0, 0)),
                  per_b(N_HEADS, t)],
        out_specs=pl.BlockSpec((1, ATT_W, tq), lambda i, j: (i, 0, j)),
        scratch_shapes=[pltpu.VMEM((N_HEADS, 2 * LANES, tq), BF16), pltpu.VMEM((ATT_W, tq), F32)],
        compiler_params=pltpu.CompilerParams(dimension_semantics=("parallel", "arbitrary"),
                                             vmem_limit_bytes=VMEM_LIMIT),
        name="attn_prompt",
    )(qt, kb, kaux, vt_all, drow)


def _back_kernel(pt_ref, *refs, alpha, mod_rep, att_transposed, side_cfg):
    (x_ref, att_ref, g0_ref, pm_ref, mod_ref, wba_ref, wo_ref, l1g_ref, l1b_ref, wgu_ref, wd_ref,
     l2g_ref, l2b_ref) = refs[:13]
    n_side = _SideAttention.N_IN if side_cfg else 0
    side_in = refs[13:13 + n_side]
    o_ref = refs[13 + n_side]
    d = D_MODEL
    side = None
    if side_cfg:
        side = _SideAttention(pt_ref, side_in, refs[13 + n_side + 1], refs[13 + n_side + 2:], side_cfg,
                              pl.program_id(0) * pl.num_programs(1) + pl.program_id(1))
        side.start()

    def mod(k):
        if mod_rep == 0:
            return mod_ref[0, :, k * d:(k + 1) * d]
        m = mod_ref[:, k * d:(k + 1) * d]
        return jnp.concatenate([m] * mod_rep, axis=0)

    x = x_ref[...].reshape(x_ref.shape[-2:])
    att = att_ref[...].reshape(att_ref.shape[-2:])
    if att_transposed:
        att = att.T
    g0 = g0_ref[...].reshape(x.shape)
    pm = pm_ref[...].reshape(x.shape)
    merged = g0 * _dot(att.astype(BF16), wba_ref[...]) + pm
    mix = _dot(merged.astype(BF16), wo_ref[...])
    x1 = _layer_norm(alpha * x + mod(2) * mix, l1g_ref[...], l1b_ref[...])
    h2 = (x1 * (1.0 + mod(4)) + mod(3)).astype(BF16)
    f = None
    for c0 in range(0, FF, FF_CHUNK):
        a = _dot(h2, wgu_ref[:, c0:c0 + FF_CHUNK])
        b = _dot(h2, wgu_ref[:, FF + c0:FF + c0 + FF_CHUNK])
        fc = _dot((_silu(a) * b).astype(BF16), wd_ref[c0:c0 + FF_CHUNK, :])
        f = fc if f is None else f + fc
    out = _layer_norm(alpha * x1 + mod(5) * f, l2g_ref[...], l2b_ref[...])
    o_ref[...] = out.reshape(o_ref.shape)

    if side:
        side.finish()


def _side_parts(side):
    if side is None:
        return [], (), (), (), [], None
    return side["in_specs"], side["args"], (side["out_spec"],), (side["out_shape"],), side["scratch"], side["cfg"]


def _back_weights(w):
    return (w["wba"], w["wo"], w["l1g"], w["l1b"], w["wgu"], w["wd"], w["l2g"], w["l2b"])


def _back_prompt(x, att, g0, pm, mod, w, pt_flat, side, *, tm, alpha):
    b, t, d = x.shape
    tile = lambda width: pl.BlockSpec((1, tm, width), lambda i, j, *_: (i, j, 0))
    weights = _back_weights(w)
    side_specs, side_args, side_out_spec, side_out_shape, side_scratch, side_cfg = _side_parts(side)
    return pl.pallas_call(
        functools.partial(_back_kernel, alpha=alpha, mod_rep=0, att_transposed=True, side_cfg=side_cfg),
        out_shape=(jax.ShapeDtypeStruct((b, t, d), F32),) + side_out_shape,
        grid_spec=pltpu.PrefetchScalarGridSpec(
            num_scalar_prefetch=1,
            grid=(b, t // tm),
            in_specs=[tile(d), pl.BlockSpec((1, ATT_W, tm), lambda i, j, *_: (i, 0, j)), tile(d), tile(d),
                      pl.BlockSpec((1, 1, 6 * d), lambda i, j, *_: (i, 0, 0))]
                     + [_const_spec(a.shape) for a in weights] + side_specs,
            out_specs=(tile(d),) + side_out_spec,
            scratch_shapes=side_scratch),
        compiler_params=pltpu.CompilerParams(dimension_semantics=("arbitrary", "arbitrary"),
                                             vmem_limit_bytes=VMEM_LIMIT),
        name="back_prompt",
    )(pt_flat, x, att, g0, pm, mod, *weights, *side_args)


def _back_sample(x, att, g0, pm, mod, w, pt_flat, *, steps_per_tile, alpha):
    rows, d = x.shape
    n_seq = mod.shape[0]
    tm = steps_per_tile * n_seq
    tile = lambda width: pl.BlockSpec((tm, width), lambda i, *_: (i, 0))
    weights = _back_weights(w)
    return pl.pallas_call(
        functools.partial(_back_kernel, alpha=alpha, mod_rep=steps_per_tile, att_transposed=False, side_cfg=None),
        out_shape=(jax.ShapeDtypeStruct((rows, d), F32),),
        grid_spec=pltpu.PrefetchScalarGridSpec(
            num_scalar_prefetch=1,
            grid=(rows // tm,),
            in_specs=[tile(d), tile(ATT_W), tile(d), tile(d), _const_spec(mod.shape)]
                     + [_const_spec(a.shape) for a in weights],
            out_specs=(tile(d),)),
        compiler_params=pltpu.CompilerParams(dimension_semantics=("parallel",), vmem_limit_bytes=VMEM_LIMIT),
        name="back_sample",
    )(pt_flat, x, att, g0, pm, mod, *weights)[0]


def _front_sample_kernel(x_ref, mod_ref, cprev_ref, pprev_ref, wqkv_ref, wf_ref, bf_ref, wc_ref, cw_ref, wp_ref,
                         pw_ref, ps_ref, wg_ref, wbc_ref, wbp_ref,
                         q_ref, k_ref, v_ref, kt_ref, vt_ref, lft_ref, g0_ref, pm_ref, u_ref, p_ref,
                         ucar, phist, *, past_len):
    d = D_MODEL
    t = pl.program_id(0)

    @pl.when(t == 0)
    def _():
        ucar[...] = cprev_ref[...]
        phist[...] = pprev_ref[...]

    sh = mod_ref[:, 0:d]
    sc = mod_ref[:, d:2 * d]
    h = (x_ref[...] * (1.0 + sc) + sh).astype(BF16)

    qkv = _dot(h, wqkv_ref[...])
    q_ref[...] = qkv[:, 0:ATT_W]
    k_ref[...] = qkv[:, ATT_W:2 * ATT_W]
    v_ref[...] = qkv[:, 2 * ATT_W:3 * ATT_W]
    kt_ref[0] = qkv[:, ATT_W:2 * ATT_W].T
    vt_ref[0] = qkv[:, 2 * ATT_W:3 * ATT_W].T

    lf = _log_sigmoid(_dot(h, wf_ref[...]) + bf_ref[...])
    lft_ref[0] = lf.T[0:N_HEADS, :]

    zc = _dot(h, wc_ref[...])
    gb = zc[:, 0:CONV_W]
    u = zc[:, CONV_W:2 * CONV_W] * zc[:, 2 * CONV_W:3 * CONV_W]
    y = cw_ref[0:1, :] * ucar[0] + cw_ref[1:2, :] * ucar[1] + cw_ref[2:3, :] * u
    conv_out = (gb * y).astype(BF16)
    ucar[0] = ucar[1]
    ucar[1] = u
    u_ref[...] = u

    p = _dot(h, wp_ref[...])
    p_ref[...] = p
    pouts = []
    for g, w in enumerate(POOL_WINDOWS):
        cols = slice(g * POOL_GW, (g + 1) * POOL_GW)
        win = p[:, cols]
        for j in range(1, w):
            win = win + phist[POOL_STATE - j, :, cols]
        cnt = jnp.minimum(past_len + t + 1, w).astype(F32)
        pooled = (win / cnt - p[:, cols]).astype(BF16)
        pouts.append(_dot(pooled, pw_ref[g]))
    pool_out = (jnp.concatenate(pouts, axis=1) * ps_ref[...]).astype(BF16)
    for k in range(POOL_STATE - 1):
        phist[k] = phist[k + 1]
    phist[POOL_STATE - 1] = p

    g_att, pm = _gated_merge_partial(h, conv_out, pool_out, wg_ref, wbc_ref, wbp_ref)
    g0_ref[...] = g_att
    pm_ref[...] = pm


def _front_sample(x, mod, cprev, pprev, w, *, n_t, past_len):
    rows, d = x.shape
    n_seq = rows // n_t
    tile = lambda width: pl.BlockSpec((n_seq, width), lambda i: (i, 0))
    tile_t = lambda r: pl.BlockSpec((1, r, n_seq), lambda i: (i, 0, 0))
    weights = (w["wqkv"], w["wf"], w["bf"], w["wc"], w["cw"], w["wp"], w["pw"], w["ps"], w["wg"], w["wbc"], w["wbp"])
    return pl.pallas_call(
        functools.partial(_front_sample_kernel, past_len=past_len),
        out_shape=(jax.ShapeDtypeStruct((rows, ATT_W), F32),
                   jax.ShapeDtypeStruct((rows, ATT_W), F32),
                   jax.ShapeDtypeStruct((rows, ATT_W), F32),
                   jax.ShapeDtypeStruct((n_t, ATT_W, n_seq), F32),
                   jax.ShapeDtypeStruct((n_t, ATT_W, n_seq), F32),
                   jax.ShapeDtypeStruct((n_t, N_HEADS, n_seq), F32),
                   jax.ShapeDtypeStruct((rows, d), F32),
                   jax.ShapeDtypeStruct((rows, d), F32),
                   jax.ShapeDtypeStruct((rows, CONV_W), F32),
                   jax.ShapeDtypeStruct((rows, POOL_W), F32)),
        grid=(n_t,),
        in_specs=[tile(d), _const_spec(mod.shape), _const_spec(cprev.shape), _const_spec(pprev.shape)]
                 + [_const_spec(a.shape) for a in weights],
        out_specs=(tile(ATT_W), tile(ATT_W), tile(ATT_W), tile_t(ATT_W), tile_t(ATT_W), tile_t(N_HEADS),
                   tile(d), tile(d), tile(CONV_W), tile(POOL_W)),
        scratch_shapes=[pltpu.VMEM((CONV_K - 1, n_seq, CONV_W), F32),
                        pltpu.VMEM((POOL_STATE, n_seq, POOL_W), F32)],
        compiler_params=pltpu.CompilerParams(dimension_semantics=("arbitrary",), vmem_limit_bytes=VMEM_LIMIT),
        name="front_sample",
    )(x, mod, cprev, pprev, *weights)


def _attn_sample_probs(q_ref, kn_ref, lfn_ref, k_refs, lf_refs):
    n_pages = len(k_refs)
    n_t = q_ref.shape[1]
    rows = n_t * N_HEADS

    q = q_ref[0] * (HEAD_DIM ** -0.5)
    head_of_lane = lax.broadcasted_iota(jnp.int32, (N_HEADS, ATT_W), 1) // HEAD_DIM
    head_mask = head_of_lane == lax.broadcasted_iota(jnp.int32, (N_HEADS, ATT_W), 0)
    qbd = jnp.concatenate(
        [jnp.where(head_mask, jnp.broadcast_to(q[t:t + 1, :], (N_HEADS, ATT_W)), 0.0) for t in range(n_t)], axis=0)

    li = lax.broadcasted_iota(jnp.int32, (PAGE, PAGE), 0)
    lj = lax.broadcasted_iota(jnp.int32, (PAGE, PAGE), 1)
    cn = _dot_hi(lfn_ref[0], jnp.where(li <= lj, 1.0, 0.0).astype(F32))
    lf_all = jnp.concatenate([lf_refs[pg][...] for pg in range(n_pages)], axis=0)
    r_in = _dot_hi(lf_all, jnp.where(li > lj, 1.0, 0.0).astype(F32))
    tot = jnp.sum(lf_all, axis=1, keepdims=True)
    carry = jnp.zeros((N_HEADS, 1), F32)
    r_pages = [None] * n_pages
    for pg in reversed(range(n_pages)):
        sl = slice(pg * N_HEADS, (pg + 1) * N_HEADS)
        r_pages[pg] = r_in[sl, :] + carry
        carry = carry + tot[sl, :]
    cn_col = jnp.concatenate([cn[:, t:t + 1] for t in range(n_t)], axis=0)

    s_pages = [_dot(qbd, k_refs[pg][...]) + jnp.concatenate([r_pages[pg]] * n_t, axis=0) + cn_col
               for pg in range(n_pages)]
    kn = jnp.concatenate([kn_ref[0], jnp.zeros((PAGE - n_t, ATT_W), F32)], axis=0)
    s_new = _dot_nt(qbd, kn) + cn_col - jnp.concatenate([cn] * n_t, axis=0)
    t_of_row = lax.broadcasted_iota(jnp.int32, (rows, PAGE), 0) // N_HEADS
    s_new = jnp.where(lax.broadcasted_iota(jnp.int32, (rows, PAGE), 1) <= t_of_row, s_new, NEG)

    m = jnp.max(s_new, axis=1, keepdims=True)
    for s in s_pages:
        m = jnp.maximum(m, jnp.max(s, axis=1, keepdims=True))
    p_new = jnp.exp(s_new - m)
    p_pages = [jnp.exp(s - m) for s in s_pages]
    l = jnp.sum(p_new, axis=1, keepdims=True)
    for p in p_pages:
        l = l + jnp.sum(p, axis=1, keepdims=True)
    return p_new, p_pages, l


def _attn_sample_output(probs, vn_ref, v_refs, o_ref):
    p_new, p_pages, l = probs
    n_t = vn_ref.shape[1]
    vn = jnp.concatenate([vn_ref[0], jnp.zeros((PAGE - n_t, ATT_W), F32)], axis=0)
    o = _dot(p_new, vn)
    for p, v_ref in zip(p_pages, v_refs):
        o = o + _dot_nt(p, v_ref[...])
    o = o / l
    head_of_lane = lax.broadcasted_iota(jnp.int32, (N_HEADS, ATT_W), 1) // HEAD_DIM
    head_mask = head_of_lane == lax.broadcasted_iota(jnp.int32, (N_HEADS, ATT_W), 0)
    for t in range(n_t):
        blk = jnp.where(head_mask, o[t * N_HEADS:(t + 1) * N_HEADS, :], 0.0)
        o_ref[0, t:t + 1, :] = jnp.sum(blk, axis=0, keepdims=True)


class _SideAttention:
    N_IN = 7
    N_SCRATCH = 4

    def __init__(self, pt_ref, ins, o_ref, scratch, cfg, step):
        self.pt_ref = pt_ref
        self.q_ref, self.kn_ref, self.vn_ref, self.lfn_ref, self.ck, self.cv, self.clf = ins
        self.o_ref = o_ref
        self.kbuf, self.vbuf, self.lfbuf, self.sems = scratch
        self.layer, self.n_pages, self.n_seq, self.seq_base, self.n_steps = cfg
        self.step = step
        self.slot = lax.rem(step, 2)
        self.probs = None

    def _copies(self, step, slot):
        seq = self.seq_base + step
        copies = []
        for pg in range(self.n_pages):
            page = self.pt_ref[pg * self.n_seq + seq]
            for kind, (src, dst) in enumerate(((self.ck, self.kbuf), (self.cv, self.vbuf), (self.clf, self.lfbuf))):
                copies.append(pltpu.make_async_copy(src.at[self.layer, page], dst.at[slot, pg],
                                                    self.sems.at[slot, kind]))
        return copies

    def start(self):
        step, slot = self.step, self.slot

        @pl.when(step == 0)
        def _():
            for cp in self._copies(step, slot):
                cp.start()

        @pl.when(step + 1 < self.n_steps)
        def _():
            for cp in self._copies(step + 1, 1 - slot):
                cp.start()

        for cp in self._copies(step, slot):
            cp.wait()
        pages = range(self.n_pages)
        self.probs = _attn_sample_probs(self.q_ref, self.kn_ref, self.lfn_ref,
                                        [self.kbuf.at[slot, pg] for pg in pages],
                                        [self.lfbuf.at[slot, pg] for pg in pages])

    def finish(self):
        _attn_sample_output(self.probs, self.vn_ref, [self.vbuf.at[self.slot, pg] for pg in range(self.n_pages)],
                            self.o_ref)


def _attn_sample_operands(q, kn, vn, lfn, ckt, cvt, clft, *, layer, n_pages, seq_base, n_steps, step_of):
    n_seq, n_t, _ = q.shape
    per_seq = lambda r, width: pl.BlockSpec((1, r, width), lambda *g: (seq_base + step_of(*g[:-1]), 0, 0))
    return dict(
        in_specs=[per_seq(n_t, ATT_W), per_seq(n_t, ATT_W), per_seq(n_t, ATT_W), per_seq(N_HEADS, PAGE)]
                 + [pl.BlockSpec(memory_space=pl.ANY)] * 3,
        args=(q, kn, vn, lfn, ckt, cvt, clft),
        out_spec=pl.BlockSpec((1, n_t, ATT_W), lambda *g: (step_of(*g[:-1]), 0, 0)),
        out_shape=jax.ShapeDtypeStruct((n_steps, n_t, ATT_W), F32),
        scratch=[pltpu.VMEM((2, n_pages, ATT_W, PAGE), F32), pltpu.VMEM((2, n_pages, ATT_W, PAGE), F32),
                 pltpu.VMEM((2, n_pages, N_HEADS, PAGE), F32), pltpu.SemaphoreType.DMA((2, 3))],
        cfg=(layer, n_pages, n_seq, seq_base, n_steps))


def _attn_sample_kernel(pt_ref, *refs, cfg):
    n_in, n_scr = _SideAttention.N_IN, _SideAttention.N_SCRATCH
    side = _SideAttention(pt_ref, refs[:n_in], refs[n_in], refs[n_in + 1:n_in + 1 + n_scr], cfg, pl.program_id(0))
    side.start()
    side.finish()


def _attn_sample(pt_flat, q, kn, vn, lfn, ckt, cvt, clft, *, layer, n_pages):
    n_seq = q.shape[0]
    side = _attn_sample_operands(q, kn, vn, lfn, ckt, cvt, clft, layer=layer, n_pages=n_pages, seq_base=0,
                                 n_steps=n_seq, step_of=lambda s: s)
    return pl.pallas_call(
        functools.partial(_attn_sample_kernel, cfg=side["cfg"]),
        out_shape=side["out_shape"],
        grid_spec=pltpu.PrefetchScalarGridSpec(
            num_scalar_prefetch=1, grid=(n_seq,), in_specs=side["in_specs"], out_specs=side["out_spec"],
            scratch_shapes=side["scratch"]),
        compiler_params=pltpu.CompilerParams(dimension_semantics=("arbitrary",), vmem_limit_bytes=VMEM_LIMIT),
        name="attn_sample",
    )(pt_flat, *side["args"])


def _layer_weights(l, w_in, b_forget, conv_w, pool_w, pool_scale, w_br_attn, w_br_conv, w_br_pool, w_o,
                   ln1_g, ln1_b, w_gate_up, w_down, ln2_g, ln2_b):
    wi = w_in[l]
    row = lambda v: v[l].reshape(1, -1)
    return dict(
        wqkv=wi[:, OFF_Q:OFF_F].astype(BF16),
        wf=jnp.pad(wi[:, OFF_F:OFF_CB], ((0, 0), (0, LANES - N_HEADS))).astype(BF16),
        bf=jnp.pad(row(b_forget), ((0, 0), (0, LANES - N_HEADS))),
        wc=wi[:, OFF_CB:OFF_P].astype(BF16),
        cw=conv_w[l],
        wp=wi[:, OFF_P:OFF_G].astype(BF16),
        pw=pool_w[l].astype(BF16),
        ps=row(pool_scale),
        wg=wi[:, OFF_G:].astype(BF16),
        wbc=w_br_conv[l].astype(BF16),
        wbp=w_br_pool[l].astype(BF16),
        wba=w_br_attn[l].astype(BF16),
        wo=w_o[l].astype(BF16),
        l1g=row(ln1_g), l1b=row(ln1_b),
        wgu=w_gate_up[l].astype(BF16),
        wd=w_down[l].astype(BF16),
        l2g=row(ln2_g), l2b=row(ln2_b),
    )


def kernel(x_prompt, x_sample, cache_k, cache_v, cache_logf, state_conv, state_pool, page_table, c_prompt, c_sample, ada_w, ada_b, w_in, b_forget, conv_w, pool_w, pool_scale, w_br_attn, w_br_conv, w_br_pool, w_o, ln1_g, ln1_b, w_gate_up, w_down, ln2_g, ln2_b):
    depth = ada_w.shape[0]
    alpha = float((2 * depth) ** 0.25)
    nb, seq, d = x_prompt.shape
    n_seq, n_t, _ = x_sample.shape
    n_pool = cache_k.shape[1]
    n_pages = page_table.shape[1]
    past_len = n_pages * PAGE
    tm = 256
    nt = seq // tm

    mod = _modulation(jnp.concatenate([c_prompt, c_sample], axis=0), ada_w, ada_b)
    mod_p = mod[:, :nb].reshape(depth, nb, 1, 6 * d)
    mod_s = mod[:, nb:]

    ckt = cache_k.transpose(0, 1, 3, 4, 2).reshape(depth, n_pool, ATT_W, PAGE)
    cvt = cache_v.transpose(0, 1, 3, 4, 2).reshape(depth, n_pool, ATT_W, PAGE)
    clft = cache_logf.transpose(0, 1, 3, 2)
    pt_flat = page_table.T.reshape(-1)

    xp = x_prompt
    xs = x_sample.transpose(1, 0, 2).reshape(n_t * n_seq, d)
    cprev_t = state_conv.transpose(0, 2, 1, 3)
    pprev_t = state_pool.transpose(0, 2, 1, 3)

    outs_p = [[] for _ in range(3)]
    outs_s = [[] for _ in range(5)]
    kv_all = None
    for l in range(depth):
        w = _layer_weights(l, w_in, b_forget, conv_w, pool_w, pool_scale, w_br_attn, w_br_conv, w_br_pool, w_o,
                           ln1_g, ln1_b, w_gate_up, w_down, ln2_g, ln2_b)
        qs, ks, vs, kts, vts, lfts, g0s, pms, us, ps = _front_sample(
            xs, mod_s[l], cprev_t[l], pprev_t[l], w, n_t=n_t, past_len=past_len)
        seq_major = lambda a: a.reshape(n_t, n_seq, -1).transpose(1, 0, 2)
        lfn = jnp.pad(lfts.transpose(2, 1, 0), ((0, 0), (0, 0), (0, PAGE - n_t)))
        sample_att_in = (seq_major(qs), seq_major(ks), seq_major(vs), lfn, ckt, cvt, clft)
        ride = n_seq == 2 * nb * nt
        side_f = side_b = None
        if ride:
            half = n_seq // 2
            step_of = lambda i, j: i * nt + j
            side_f = _attn_sample_operands(*sample_att_in, layer=l, n_pages=n_pages, seq_base=0, n_steps=half,
                                           step_of=step_of)
            side_b = _attn_sample_operands(*sample_att_in, layer=l, n_pages=n_pages, seq_base=half, n_steps=half,
                                           step_of=step_of)

        front = _front_prompt(xp, mod_p[l], w, kv_all, pt_flat, side_f, tm=tm, layer=l, depth=depth)
        qt, kb, kaux, kt_all, vt_all, lft, drow, g0, pm, cst, pst = front[:11]
        kv_all = (kt_all, vt_all)
        att = _attn_prompt(qt, kb, kaux, vt_all, drow, tq=tm, layer=l)
        back = _back_prompt(xp, att, g0, pm, mod_p[l], w, pt_flat, side_b, tm=tm, alpha=alpha)
        xp = back[0]
        outs_p[0].append(lft.transpose(0, 2, 1))
        outs_p[1].append(cst[:, CONV_HIST - (CONV_K - 1):])
        outs_p[2].append(pst[:, POOL_HIST - POOL_STATE:])

        if ride:
            att_s = jnp.concatenate([front[11], back[1]], axis=0)
        else:
            att_s = _attn_sample(pt_flat, *sample_att_in, layer=l, n_pages=n_pages)
        att_t = att_s.transpose(1, 0, 2).reshape(n_t * n_seq, ATT_W)
        xs = _back_sample(xs, att_t, g0s, pms, mod_s[l], w, pt_flat, steps_per_tile=2, alpha=alpha)
        outs_s[0].append(kts.reshape(n_t, N_HEADS, HEAD_DIM, n_seq).transpose(3, 0, 1, 2))
        outs_s[1].append(vts.reshape(n_t, N_HEADS, HEAD_DIM, n_seq).transpose(3, 0, 1, 2))
        outs_s[2].append(lfts.transpose(2, 0, 1))
        outs_s[3].append(seq_major(us)[:, n_t - (CONV_K - 1):])
        outs_s[4].append(jnp.concatenate([state_pool[l], seq_major(ps)], axis=1)[:, -POOL_STATE:])

    y_sample = xs.reshape(n_t, n_seq, d).transpose(1, 0, 2)
    kv_p = tuple(a.reshape(depth, nb, N_HEADS, HEAD_DIM, seq).transpose(0, 1, 4, 2, 3) for a in kv_all)
    return (xp, y_sample) + kv_p + tuple(jnp.stack(a) for a in outs_p) + tuple(jnp.stack(a) for a in outs_s)
```

```python
import functools

import jax
import jax.numpy as jnp
from jax import lax
from jax.experimental import pallas as pl
from jax.experimental.pallas import tpu as pltpu

F32 = jnp.float32
BF16 = jnp.bfloat16

D_MODEL = 1024
N_HEADS = 8
HEAD_DIM = 64
ATT_W = N_HEADS * HEAD_DIM
CONV_W = 512
CONV_K = 3
POOL_WINDOWS = (2, 4, 8, 16)
POOL_GW = 128
POOL_W = len(POOL_WINDOWS) * POOL_GW
POOL_STATE = max(POOL_WINDOWS) - 1
POOL_HIST = POOL_STATE + 1
CONV_HIST = 8
FF = 2816
FF_CHUNK = 1408
LN_EPS = 1e-5
PAGE = 128
LANES = 128
NEG = -1e30
QK_AHEAD = 8
LOG2E = 1.4426950408889634

OFF_Q = 0
OFF_F = 3 * ATT_W
OFF_CB = OFF_F + N_HEADS
OFF_P = OFF_CB + 3 * CONV_W
OFF_G = OFF_P + POOL_W

VMEM_LIMIT = 58 * 1024 * 1024


def _dot(a, b):
    return jnp.dot(a, b, preferred_element_type=F32)


def _dot_nt(a, b):
    return lax.dot_general(a, b, (((1,), (1,)), ((), ())), preferred_element_type=F32)


def _dot_hi(a, b):
    return jnp.dot(a, b, preferred_element_type=F32, precision=lax.Precision.HIGHEST)


def _log_sigmoid(x):
    return jnp.minimum(x, 0.0) - jnp.log1p(jnp.exp(-jnp.abs(x)))


def _silu(x):
    return x * jax.nn.sigmoid(x)


def _layer_norm(x, g, b):
    mu = jnp.mean(x, axis=-1, keepdims=True)
    xc = x - mu
    var = jnp.mean(xc * xc, axis=-1, keepdims=True)
    return xc * lax.rsqrt(var + LN_EPS) * g + b


def _const_spec(shape):
    nd = len(shape)
    return pl.BlockSpec(shape, lambda *_: (0,) * nd, pipeline_mode=pl.Buffered(1))


def _mod_kernel(cp_ref, cs_ref, w_ref, b_ref, op_ref, os_ref):
    w = w_ref[0].astype(BF16)
    for c_ref, o_ref in ((cp_ref, op_ref), (cs_ref, os_ref)):
        o_ref[0] = _dot(_silu(c_ref[...]).astype(BF16), w) + b_ref[0]


def _modulation(c_prompt, c_sample, ada_w, ada_b):
    depth, d, n = ada_w.shape
    tn = 1536
    rows = lambda c: pl.BlockSpec(c.shape, lambda l, j: (0, 0))
    out = lambda c: pl.BlockSpec((1, c.shape[0], tn), lambda l, j: (l, 0, j))
    return pl.pallas_call(
        _mod_kernel,
        out_shape=tuple(jax.ShapeDtypeStruct((depth, c.shape[0], n), F32) for c in (c_prompt, c_sample)),
        grid=(depth, n // tn),
        in_specs=[rows(c_prompt), rows(c_sample),
                  pl.BlockSpec((1, d, tn), lambda l, j: (l, 0, j)),
                  pl.BlockSpec((1, 1, tn), lambda l, j: (l, 0, j))],
        out_specs=(out(c_prompt), out(c_sample)),
        compiler_params=pltpu.CompilerParams(vmem_limit_bytes=VMEM_LIMIT),
        name="modulation",
    )(c_prompt, c_sample, ada_w, ada_b.reshape(depth, 1, n))


def _gate_logits(h, wg_ref):
    d = D_MODEL
    return tuple(_dot(h, wg_ref[:, k * d:(k + 1) * d]) for k in range(3))


def _gated_merge_partial(zg, conv_out, pool_out, wbc_ref, wbp_ref):
    g_att, g_conv, g_pool = (jax.nn.sigmoid(z) for z in zg)
    pm = g_conv * _dot(conv_out, wbc_ref[...]) + g_pool * _dot(pool_out, wbp_ref[...])
    return g_att, pm


def _split3(x):
    hi = x.astype(BF16).astype(F32)
    r = x - hi
    mid = r.astype(BF16).astype(F32)
    lo = (r - mid).astype(BF16).astype(F32)
    return hi, mid, lo


def _front_prompt_kernel(pt_ref, *refs, tm, layer, n_alias, side_cfg):
    (x_ref, mod_ref, wqkv_ref, wf_ref, bf_ref, wc_ref, cw_ref, wp_ref, pw_ref, ps_ref,
     wg_ref, wbc_ref, wbp_ref) = refs[:13]
    n_side = _SideAttention.N_IN if side_cfg else 0
    side_in = refs[13 + n_alias:13 + n_alias + n_side]
    outs = refs[13 + n_alias + n_side:]
    qt_ref, kb_ref, kaux_ref, kt_ref, vt_ref, lft_ref, drow_ref, g0_ref, pm_ref, cst_ref, pst_ref = outs[:11]
    scratch = outs[11 + (1 if side_cfg else 0):]
    ucar, pcar, dcar = scratch[:3]
    d = D_MODEL
    ti = pl.program_id(1)
    side = None
    if side_cfg:
        side = _SideAttention(pt_ref, side_in, outs[11], scratch[3:], side_cfg,
                              pl.program_id(0) * pl.num_programs(1) + ti)
        side.start()

    @pl.when(ti == 0)
    def _():
        ucar[...] = jnp.zeros_like(ucar)
        pcar[...] = jnp.zeros_like(pcar)
        dcar[...] = jnp.zeros_like(dcar)

    x = x_ref[0]
    sh = mod_ref[0, :, 0:d]
    sc = mod_ref[0, :, d:2 * d]
    h = (x * (1.0 + sc) + sh).astype(BF16)

    qkv = _dot(h, wqkv_ref[...])
    qt_ref[0] = (qkv[:, 0:ATT_W] * (HEAD_DIM ** -0.5 * LOG2E)).T.astype(BF16)
    kb_ref[0] = qkv[:, ATT_W:2 * ATT_W].astype(BF16)
    own = layer if n_alias == 0 else 0
    for slab in range(kt_ref.shape[0]):
        if slab == own:
            kt_ref[slab, 0] = qkv[:, ATT_W:2 * ATT_W].T
            vt_ref[slab, 0] = qkv[:, 2 * ATT_W:3 * ATT_W].T
        else:
            kt_ref[slab, 0] = jnp.zeros(kt_ref.shape[2:], F32)
            vt_ref[slab, 0] = jnp.zeros(vt_ref.shape[2:], F32)

    lf = _log_sigmoid(_dot(h, wf_ref[...]) + bf_ref[...])
    r = lax.broadcasted_iota(jnp.int32, (tm, tm), 0)
    c = lax.broadcasted_iota(jnp.int32, (tm, tm), 1)
    tri = jnp.where(c <= r, 1.0, 0.0).astype(F32)
    dcum = _dot_hi(tri, lf) + dcar[...]
    dcar[...] = dcum[tm - 1:tm, :]
    lft_ref[0] = lf.T[0:N_HEADS, :]
    d2 = dcum * LOG2E
    drow_ref[0] = d2.T[0:N_HEADS, :]
    hi, mid, lo = _split3(-d2)
    lane = lax.broadcasted_iota(jnp.int32, (tm, LANES), 1)
    aux = jnp.where(lane < N_HEADS, hi,
                    jnp.where(lane < 2 * N_HEADS, pltpu.roll(mid, N_HEADS, axis=1),
                              jnp.where(lane < 3 * N_HEADS, pltpu.roll(lo, 2 * N_HEADS, axis=1),
                                        jnp.where(lane < 3 * N_HEADS + 3, 1.0, 0.0))))
    kaux_ref[0] = aux.astype(BF16)

    zc = _dot(h, wc_ref[...])
    gb = zc[:, 0:CONV_W]
    u = zc[:, CONV_W:2 * CONV_W] * zc[:, 2 * CONV_W:3 * CONV_W]
    uext = jnp.concatenate([ucar[...], u], axis=0)
    u1 = pltpu.roll(uext, 1, axis=0)[CONV_HIST:, :]
    u2 = pltpu.roll(uext, 2, axis=0)[CONV_HIST:, :]
    y = cw_ref[0:1, :] * u2 + cw_ref[1:2, :] * u1 + cw_ref[2:3, :] * u
    conv_out = (gb * y).astype(BF16)
    ucar[...] = u[tm - CONV_HIST:, :]
    cst_ref[0] = u[tm - CONV_HIST:, :]

    p = _dot(h, wp_ref[...])
    pext = jnp.concatenate([pcar[...], p], axis=0)
    pos1 = ti * tm + lax.broadcasted_iota(jnp.int32, (tm, POOL_GW), 0) + 1
    pouts = []
    for g, w in enumerate(POOL_WINDOWS):
        cols = slice(g * POOL_GW, (g + 1) * POOL_GW)
        s = pext[:, cols]
        k = 1
        while k < w:
            s = s + pltpu.roll(s, k, axis=0)
            k *= 2
        cnt = jnp.minimum(pos1, w).astype(F32)
        pooled = (s[POOL_HIST:, :] / cnt - p[:, cols]).astype(BF16)
        pouts.append(_dot(pooled, pw_ref[g]))
    pool_out = (jnp.concatenate(pouts, axis=1) * ps_ref[...]).astype(BF16)
    pcar[...] = p[tm - POOL_HIST:, :]
    pst_ref[0] = p[tm - POOL_HIST:, :]

    g_att, pm = _gated_merge_partial(_gate_logits(h, wg_ref), conv_out, pool_out, wbc_ref, wbp_ref)
    g0_ref[0] = g_att
    pm_ref[0] = pm

    if side:
        side.finish()


def _front_prompt(x, mod, w, kv_all, pt_flat, side, *, tm, layer, depth):
    b, t, d = x.shape
    nt = t // tm
    tile = lambda width: pl.BlockSpec((1, tm, width), lambda i, j, *_: (i, j, 0))
    tile_t = lambda rows: pl.BlockSpec((1, rows, tm), lambda i, j, *_: (i, 0, j))
    per_b = lambda rows, width: pl.BlockSpec((1, rows, width), lambda i, j, *_: (i, 0, 0))
    if kv_all is None:
        slab_t = pl.BlockSpec((depth, 1, ATT_W, tm), lambda i, j, *_: (0, i, 0, j))
    else:
        slab_t = pl.BlockSpec((1, 1, ATT_W, tm), lambda i, j, *_: (layer, i, 0, j))
    weights = (w["wqkv"], w["wf"], w["bf"], w["wc"], w["cw"], w["wp"], w["pw"], w["ps"], w["wg"], w["wbc"], w["wbp"])
    aliased = () if kv_all is None else tuple(kv_all)
    n_in = 1 + 2 + len(weights)
    side_specs, side_args, side_out_spec, side_out_shape, side_scratch, side_cfg = _side_parts(side)
    return pl.pallas_call(
        functools.partial(_front_prompt_kernel, tm=tm, layer=layer, n_alias=len(aliased), side_cfg=side_cfg),
        out_shape=(jax.ShapeDtypeStruct((b, ATT_W, t), BF16),
                   jax.ShapeDtypeStruct((b, t, ATT_W), BF16),
                   jax.ShapeDtypeStruct((b, t, LANES), BF16),
                   jax.ShapeDtypeStruct((depth, b, ATT_W, t), F32),
                   jax.ShapeDtypeStruct((depth, b, ATT_W, t), F32),
                   jax.ShapeDtypeStruct((b, N_HEADS, t), F32),
                   jax.ShapeDtypeStruct((b, N_HEADS, t), F32),
                   jax.ShapeDtypeStruct((b, t, d), F32),
                   jax.ShapeDtypeStruct((b, t, d), F32),
                   jax.ShapeDtypeStruct((b, CONV_HIST, CONV_W), F32),
                   jax.ShapeDtypeStruct((b, POOL_HIST, POOL_W), F32)) + side_out_shape,
        grid_spec=pltpu.PrefetchScalarGridSpec(
            num_scalar_prefetch=1,
            grid=(b, nt),
            in_specs=[tile(d), pl.BlockSpec((1, 1, 6 * d), lambda i, j, *_: (i, 0, 0))]
                     + [_const_spec(a.shape) for a in weights]
                     + [pl.BlockSpec(memory_space=pl.ANY)] * len(aliased) + side_specs,
            out_specs=(tile_t(ATT_W), tile(ATT_W), tile(LANES), slab_t, slab_t, tile_t(N_HEADS), tile_t(N_HEADS),
                       tile(d), tile(d), per_b(CONV_HIST, CONV_W), per_b(POOL_HIST, POOL_W)) + side_out_spec,
            scratch_shapes=[pltpu.VMEM((CONV_HIST, CONV_W), F32), pltpu.VMEM((POOL_HIST, POOL_W), F32),
                            pltpu.VMEM((1, LANES), F32)] + side_scratch),
        input_output_aliases={n_in + k: 3 + k for k in range(len(aliased))},
        compiler_params=pltpu.CompilerParams(dimension_semantics=("arbitrary", "arbitrary"),
                                             vmem_limit_bytes=VMEM_LIMIT),
        name="front_prompt",
    )(pt_flat, x, mod, *weights, *aliased, *side_args)


def _attn_prompt_kernel(qt_ref, kb_ref, kaux_ref, vt_ref, drow_ref, o_ref, w_scr, acc_scr, *, tq):
    qi = pl.program_id(1)
    q0 = pl.multiple_of(qi * tq, tq)
    zeros = jnp.zeros((HEAD_DIM, tq), BF16)
    brow = lax.broadcasted_iota(jnp.int32, (LANES, tq), 0)
    dq = drow_ref[0, :, pl.ds(q0, tq)]
    for hd in range(N_HEADS):
        qh = qt_ref[0, hd * HEAD_DIM:(hd + 1) * HEAD_DIM, :]
        top = jnp.concatenate([qh, zeros] if hd % 2 == 0 else [zeros, qh], axis=0)
        hi, mid, lo = _split3(dq[hd:hd + 1, :])
        bias = jnp.where(brow == 3 * N_HEADS, hi, jnp.where(brow == 3 * N_HEADS + 1, mid,
                         jnp.where(brow == 3 * N_HEADS + 2, lo, 0.0)))
        for part in range(3):
            bias = jnp.where(brow == part * N_HEADS + hd, 1.0, bias)
        w_scr[hd] = jnp.concatenate([top, bias.astype(BF16)], axis=0)
    causal = lax.broadcasted_iota(jnp.int32, (tq, tq), 0) <= lax.broadcasted_iota(jnp.int32, (tq, tq), 1)

    def block(k0, stats):
        kaux = kaux_ref[0, pl.ds(k0, tq), :]

        def logits(hd):
            hp = hd // 2
            k2 = kb_ref[0, pl.ds(k0, tq), hp * LANES:(hp + 1) * LANES]
            return _dot(jnp.concatenate([k2, kaux], axis=1), w_scr[hd])

        pending = {hd: logits(hd) for hd in range(QK_AHEAD)}
        new_stats = []
        for hd in range(N_HEADS):
            if hd + QK_AHEAD < N_HEADS:
                pending[hd + QK_AHEAD] = logits(hd + QK_AHEAD)
            s = pending.pop(hd)
            rows = slice(hd * HEAD_DIM, (hd + 1) * HEAD_DIM)
            vt = vt_ref[0, 0, rows, pl.ds(k0, tq)].astype(BF16)
            if stats is None:
                s = jnp.where(causal, s, NEG)
                m_new = jnp.max(s, axis=0, keepdims=True)
                p = jnp.exp2(s - m_new)
                l_new = jnp.sum(p, axis=0, keepdims=True)
                acc_scr[rows, :] = _dot(vt, p.astype(BF16))
            else:
                m_old, l_old = stats[hd]
                m_new = jnp.maximum(m_old, jnp.max(s, axis=0, keepdims=True))
                a = jnp.exp2(m_old - m_new)
                p = jnp.exp2(s - m_new)
                l_new = a * l_old + jnp.sum(p, axis=0, keepdims=True)
                acc_scr[rows, :] = a * acc_scr[rows, :] + _dot(vt, p.astype(BF16))
            new_stats.append((m_new, l_new))
        return tuple(new_stats)

    stats = lax.fori_loop(0, qi, lambda j, st: block(pl.multiple_of(j * tq, tq), st), block(q0, None))
    for hd in range(N_HEADS):
        rows = slice(hd * HEAD_DIM, (hd + 1) * HEAD_DIM)
        o_ref[0, rows, :] = acc_scr[rows, :] / stats[hd][1]


def _attn_prompt(qt, kb, kaux, vt_all, drow, *, tq, layer):
    b, _, t = qt.shape
    per_b = lambda rows, width: pl.BlockSpec((1, rows, width), lambda i, j: (i, 0, 0))
    return pl.pallas_call(
        functools.partial(_attn_prompt_kernel, tq=tq),
        out_shape=jax.ShapeDtypeStruct((b, ATT_W, t), F32),
        grid=(b, t // tq),
        in_specs=[pl.BlockSpec((1, ATT_W, tq), lambda i, j: (i, 0, j)),
                  per_b(t, ATT_W), per_b(t, LANES),
                  pl.BlockSpec((1, 1, ATT_W, t), lambda i, j: (layer, i, 0, 0)),
                  per_b(N_HEADS, t)],
        out_specs=pl.BlockSpec((1, ATT_W, tq), lambda i, j: (i, 0, j)),
        scratch_shapes=[pltpu.VMEM((N_HEADS, 2 * LANES, tq), BF16), pltpu.VMEM((ATT_W, tq), F32)],
        compiler_params=pltpu.CompilerParams(dimension_semantics=("parallel", "arbitrary"),
                                             vmem_limit_bytes=VMEM_LIMIT),
        name="attn_prompt",
    )(qt, kb, kaux, vt_all, drow)


def _back_kernel(pt_ref, *refs, alpha, mod_rep, att_transposed, side_cfg):
    (x_ref, att_ref, g0_ref, pm_ref, mod_ref, wba_ref, wo_ref, l1g_ref, l1b_ref, wgu_ref, wd_ref,
     l2g_ref, l2b_ref) = refs[:13]
    n_side = _SideAttention.N_IN if side_cfg else 0
    side_in = refs[13:13 + n_side]
    o_ref = refs[13 + n_side]
    d = D_MODEL
    side = None
    if side_cfg:
        side = _SideAttention(pt_ref, side_in, refs[13 + n_side + 1], refs[13 + n_side + 2:], side_cfg,
                              pl.program_id(0) * pl.num_programs(1) + pl.program_id(1))

    def mod(k):
        if mod_rep == 0:
            return mod_ref[0, :, k * d:(k + 1) * d]
        m = mod_ref[:, k * d:(k + 1) * d]
        return jnp.concatenate([m] * mod_rep, axis=0)

    x = x_ref[...].reshape(x_ref.shape[-2:])
    att = att_ref[...].reshape(att_ref.shape[-2:])
    if att_transposed:
        att = att.T
    g0 = g0_ref[...].reshape(x.shape)
    pm = pm_ref[...].reshape(x.shape)
    merged = g0 * _dot(att.astype(BF16), wba_ref[...]) + pm
    mix = _dot(merged.astype(BF16), wo_ref[...])
    if side:
        side.start()
    x1 = _layer_norm(alpha * x + mod(2) * mix, l1g_ref[...], l1b_ref[...])
    h2 = (x1 * (1.0 + mod(4)) + mod(3)).astype(BF16)
    chunks = range(0, FF, FF_CHUNK)
    gate_up = [(_dot(h2, wgu_ref[:, c0:c0 + FF_CHUNK]), _dot(h2, wgu_ref[:, FF + c0:FF + c0 + FF_CHUNK]))
               for c0 in chunks]
    f = None
    for c0, (a, b) in zip(chunks, gate_up):
        fc = _dot((_silu(a) * b).astype(BF16), wd_ref[c0:c0 + FF_CHUNK, :])
        f = fc if f is None else f + fc
    out = _layer_norm(alpha * x1 + mod(5) * f, l2g_ref[...], l2b_ref[...])
    o_ref[...] = out.reshape(o_ref.shape)

    if side:
        side.finish()


def _side_parts(side):
    if side is None:
        return [], (), (), (), [], None
    return side["in_specs"], side["args"], (side["out_spec"],), (side["out_shape"],), side["scratch"], side["cfg"]


def _back_weights(w):
    return (w["wba"], w["wo"], w["l1g"], w["l1b"], w["wgu"], w["wd"], w["l2g"], w["l2b"])


def _back_prompt(x, att, g0, pm, mod, w, pt_flat, side, *, tm, alpha):
    b, t, d = x.shape
    tile = lambda width: pl.BlockSpec((1, tm, width), lambda i, j, *_: (i, j, 0))
    weights = _back_weights(w)
    side_specs, side_args, side_out_spec, side_out_shape, side_scratch, side_cfg = _side_parts(side)
    return pl.pallas_call(
        functools.partial(_back_kernel, alpha=alpha, mod_rep=0, att_transposed=True, side_cfg=side_cfg),
        out_shape=(jax.ShapeDtypeStruct((b, t, d), F32),) + side_out_shape,
        grid_spec=pltpu.PrefetchScalarGridSpec(
            num_scalar_prefetch=1,
            grid=(b, t // tm),
            in_specs=[tile(d), pl.BlockSpec((1, ATT_W, tm), lambda i, j, *_: (i, 0, j)), tile(d), tile(d),
                      pl.BlockSpec((1, 1, 6 * d), lambda i, j, *_: (i, 0, 0))]
                     + [_const_spec(a.shape) for a in weights] + side_specs,
            out_specs=(tile(d),) + side_out_spec,
            scratch_shapes=side_scratch),
        compiler_params=pltpu.CompilerParams(dimension_semantics=("arbitrary", "arbitrary"),
                                             vmem_limit_bytes=VMEM_LIMIT),
        name="back_prompt",
    )(pt_flat, x, att, g0, pm, mod, *weights, *side_args)


def _back_sample(x, att, g0, pm, mod, w, pt_flat, *, steps_per_tile, alpha):
    rows, d = x.shape
    n_seq = mod.shape[0]
    tm = steps_per_tile * n_seq
    tile = lambda width: pl.BlockSpec((tm, width), lambda i, *_: (i, 0))
    weights = _back_weights(w)
    return pl.pallas_call(
        functools.partial(_back_kernel, alpha=alpha, mod_rep=steps_per_tile, att_transposed=False, side_cfg=None),
        out_shape=(jax.ShapeDtypeStruct((rows, d), F32),),
        grid_spec=pltpu.PrefetchScalarGridSpec(
            num_scalar_prefetch=1,
            grid=(rows // tm,),
            in_specs=[tile(d), tile(ATT_W), tile(d), tile(d), _const_spec(mod.shape)]
                     + [_const_spec(a.shape) for a in weights],
            out_specs=(tile(d),)),
        compiler_params=pltpu.CompilerParams(dimension_semantics=("parallel",), vmem_limit_bytes=VMEM_LIMIT),
        name="back_sample",
    )(pt_flat, x, att, g0, pm, mod, *weights)[0]


def _front_sample_kernel(x_ref, mod_ref, cprev_ref, pprev_ref, wqkv_ref, wf_ref, bf_ref, wc_ref, cw_ref, wp_ref,
                         pw_ref, ps_ref, wg_ref, wbc_ref, wbp_ref,
                         q_ref, k_ref, v_ref, kt_ref, vt_ref, lft_ref, g0_ref, pm_ref, u_ref, p_ref,
                         ucar, phist, *, past_len):
    d = D_MODEL
    t = pl.program_id(0)

    @pl.when(t == 0)
    def _():
        ucar[...] = cprev_ref[...]
        phist[...] = pprev_ref[...]

    sh = mod_ref[:, 0:d]
    sc = mod_ref[:, d:2 * d]
    h = (x_ref[...] * (1.0 + sc) + sh).astype(BF16)

    qkv = _dot(h, wqkv_ref[...])
    q_ref[...] = qkv[:, 0:ATT_W]
    k_ref[...] = qkv[:, ATT_W:2 * ATT_W]
    v_ref[...] = qkv[:, 2 * ATT_W:3 * ATT_W]
    kt_ref[0] = qkv[:, ATT_W:2 * ATT_W].T
    vt_ref[0] = qkv[:, 2 * ATT_W:3 * ATT_W].T

    lf = _log_sigmoid(_dot(h, wf_ref[...]) + bf_ref[...])
    lft_ref[0] = lf.T[0:N_HEADS, :]

    zc = _dot(h, wc_ref[...])
    gb = zc[:, 0:CONV_W]
    u = zc[:, CONV_W:2 * CONV_W] * zc[:, 2 * CONV_W:3 * CONV_W]
    y = cw_ref[0:1, :] * ucar[0] + cw_ref[1:2, :] * ucar[1] + cw_ref[2:3, :] * u
    conv_out = (gb * y).astype(BF16)
    ucar[0] = ucar[1]
    ucar[1] = u
    u_ref[...] = u

    p = _dot(h, wp_ref[...])
    p_ref[...] = p
    pouts = []
    for g, w in enumerate(POOL_WINDOWS):
        cols = slice(g * POOL_GW, (g + 1) * POOL_GW)
        win = p[:, cols]
        for j in range(1, w):
            win = win + phist[POOL_STATE - j, :, cols]
        cnt = jnp.minimum(past_len + t + 1, w).astype(F32)
        pooled = (win / cnt - p[:, cols]).astype(BF16)
        pouts.append(_dot(pooled, pw_ref[g]))
    pool_out = (jnp.concatenate(pouts, axis=1) * ps_ref[...]).astype(BF16)
    for k in range(POOL_STATE - 1):
        phist[k] = phist[k + 1]
    phist[POOL_STATE - 1] = p

    g_att, pm = _gated_merge_partial(_gate_logits(h, wg_ref), conv_out, pool_out, wbc_ref, wbp_ref)
    g0_ref[...] = g_att
    pm_ref[...] = pm


def _front_sample(x, mod, cprev, pprev, w, *, n_t, past_len):
    rows, d = x.shape
    n_seq = rows // n_t
    tile = lambda width: pl.BlockSpec((n_seq, width), lambda i: (i, 0))
    tile_t = lambda r: pl.BlockSpec((1, r, n_seq), lambda i: (i, 0, 0))
    weights = (w["wqkv"], w["wf"], w["bf"], w["wc"], w["cw"], w["wp"], w["pw"], w["ps"], w["wg"], w["wbc"], w["wbp"])
    return pl.pallas_call(
        functools.partial(_front_sample_kernel, past_len=past_len),
        out_shape=(jax.ShapeDtypeStruct((rows, ATT_W), F32),
                   jax.ShapeDtypeStruct((rows, ATT_W), F32),
                   jax.ShapeDtypeStruct((rows, ATT_W), F32),
                   jax.ShapeDtypeStruct((n_t, ATT_W, n_seq), F32),
                   jax.ShapeDtypeStruct((n_t, ATT_W, n_seq), F32),
                   jax.ShapeDtypeStruct((n_t, N_HEADS, n_seq), F32),
                   jax.ShapeDtypeStruct((rows, d), F32),
                   jax.ShapeDtypeStruct((rows, d), F32),
                   jax.ShapeDtypeStruct((rows, CONV_W), F32),
                   jax.ShapeDtypeStruct((rows, POOL_W), F32)),
        grid=(n_t,),
        in_specs=[tile(d), _const_spec(mod.shape), _const_spec(cprev.shape), _const_spec(pprev.shape)]
                 + [_const_spec(a.shape) for a in weights],
        out_specs=(tile(ATT_W), tile(ATT_W), tile(ATT_W), tile_t(ATT_W), tile_t(ATT_W), tile_t(N_HEADS),
                   tile(d), tile(d), tile(CONV_W), tile(POOL_W)),
        scratch_shapes=[pltpu.VMEM((CONV_K - 1, n_seq, CONV_W), F32),
                        pltpu.VMEM((POOL_STATE, n_seq, POOL_W), F32)],
        compiler_params=pltpu.CompilerParams(dimension_semantics=("arbitrary",), vmem_limit_bytes=VMEM_LIMIT),
        name="front_sample",
    )(x, mod, cprev, pprev, *weights)


def _attn_sample_probs(q_ref, kn_ref, lfn_ref, k_refs, lf_refs):
    n_pages = len(k_refs)
    n_t = q_ref.shape[1]
    rows = n_t * N_HEADS

    q = q_ref[0] * (HEAD_DIM ** -0.5)
    head_of_lane = lax.broadcasted_iota(jnp.int32, (N_HEADS, ATT_W), 1) // HEAD_DIM
    head_mask = head_of_lane == lax.broadcasted_iota(jnp.int32, (N_HEADS, ATT_W), 0)
    qbd = jnp.concatenate(
        [jnp.where(head_mask, jnp.broadcast_to(q[t:t + 1, :], (N_HEADS, ATT_W)), 0.0) for t in range(n_t)], axis=0)

    li = lax.broadcasted_iota(jnp.int32, (PAGE, PAGE), 0)
    lj = lax.broadcasted_iota(jnp.int32, (PAGE, PAGE), 1)
    cn = _dot_hi(lfn_ref[0], jnp.where(li <= lj, 1.0, 0.0).astype(F32))
    lf_all = jnp.concatenate([lf_refs[pg][...] for pg in range(n_pages)], axis=0)
    r_in = _dot_hi(lf_all, jnp.where(li > lj, 1.0, 0.0).astype(F32))
    tot = jnp.sum(lf_all, axis=1, keepdims=True)
    carry = jnp.zeros((N_HEADS, 1), F32)
    r_pages = [None] * n_pages
    for pg in reversed(range(n_pages)):
        sl = slice(pg * N_HEADS, (pg + 1) * N_HEADS)
        r_pages[pg] = r_in[sl, :] + carry
        carry = carry + tot[sl, :]
    cn_col = jnp.concatenate([cn[:, t:t + 1] for t in range(n_t)], axis=0)

    s_pages = [_dot(qbd, k_refs[pg][...]) + jnp.concatenate([r_pages[pg]] * n_t, axis=0) + cn_col
               for pg in range(n_pages)]
    kn = jnp.concatenate([kn_ref[0], jnp.zeros((PAGE - n_t, ATT_W), F32)], axis=0)
    s_new = _dot_nt(qbd, kn) + cn_col - jnp.concatenate([cn] * n_t, axis=0)
    t_of_row = lax.broadcasted_iota(jnp.int32, (rows, PAGE), 0) // N_HEADS
    s_new = jnp.where(lax.broadcasted_iota(jnp.int32, (rows, PAGE), 1) <= t_of_row, s_new, NEG)

    m = jnp.max(s_new, axis=1, keepdims=True)
    for s in s_pages:
        m = jnp.maximum(m, jnp.max(s, axis=1, keepdims=True))
    p_new = jnp.exp(s_new - m)
    p_pages = [jnp.exp(s - m) for s in s_pages]
    l = jnp.sum(p_new, axis=1, keepdims=True)
    for p in p_pages:
        l = l + jnp.sum(p, axis=1, keepdims=True)
    return p_new, p_pages, l


def _attn_sample_output(probs, vn_ref, v_refs, o_ref):
    p_new, p_pages, l = probs
    n_t = vn_ref.shape[1]
    vn = jnp.concatenate([vn_ref[0], jnp.zeros((PAGE - n_t, ATT_W), F32)], axis=0)
    o = _dot(p_new, vn)
    for p, v_ref in zip(p_pages, v_refs):
        o = o + _dot_nt(p, v_ref[...])
    o = o / l
    head_of_lane = lax.broadcasted_iota(jnp.int32, (N_HEADS, ATT_W), 1) // HEAD_DIM
    head_mask = head_of_lane == lax.broadcasted_iota(jnp.int32, (N_HEADS, ATT_W), 0)
    for t in range(n_t):
        blk = jnp.where(head_mask, o[t * N_HEADS:(t + 1) * N_HEADS, :], 0.0)
        o_ref[0, t:t + 1, :] = jnp.sum(blk, axis=0, keepdims=True)


class _SideAttention:
    N_IN = 7
    N_SCRATCH = 4

    def __init__(self, pt_ref, ins, o_ref, scratch, cfg, step):
        self.pt_ref = pt_ref
        self.q_ref, self.kn_ref, self.vn_ref, self.lfn_ref, self.ck, self.cv, self.clf = ins
        self.o_ref = o_ref
        self.kbuf, self.vbuf, self.lfbuf, self.sems = scratch
        self.layer, self.n_pages, self.n_seq, self.seq_base, self.n_steps = cfg
        self.step = step
        self.slot = lax.rem(step, 2)
        self.probs = None

    def _copies(self, step, slot):
        seq = self.seq_base + step
        copies = []
        for pg in range(self.n_pages):
            page = self.pt_ref[pg * self.n_seq + seq]
            for kind, (src, dst) in enumerate(((self.ck, self.kbuf), (self.cv, self.vbuf), (self.clf, self.lfbuf))):
                copies.append(pltpu.make_async_copy(src.at[self.layer, page], dst.at[slot, pg],
                                                    self.sems.at[slot, kind]))
        return copies

    def _wait_slot(self, slot):
        for kind, buf in enumerate((self.kbuf, self.vbuf, self.lfbuf)):
            pltpu.make_async_copy(buf.at[slot], buf.at[slot], self.sems.at[slot, kind]).wait()

    def start(self):
        step, slot = self.step, self.slot

        @pl.when(step == 0)
        def _():
            for cp in self._copies(step, slot):
                cp.start()

        self._wait_slot(slot)
        for cp in self._copies(jnp.minimum(step + 1, self.n_steps - 1), 1 - slot):
            cp.start()
        pages = range(self.n_pages)
        self.probs = _attn_sample_probs(self.q_ref, self.kn_ref, self.lfn_ref,
                                        [self.kbuf.at[slot, pg] for pg in pages],
                                        [self.lfbuf.at[slot, pg] for pg in pages])

    def finish(self):
        _attn_sample_output(self.probs, self.vn_ref, [self.vbuf.at[self.slot, pg] for pg in range(self.n_pages)],
                            self.o_ref)

        @pl.when(self.step == self.n_steps - 1)
        def _():
            self._wait_slot(1 - self.slot)


def _attn_sample_operands(q, kn, vn, lfn, ckt, cvt, clft, *, layer, n_pages, seq_base, n_steps, step_of):
    n_seq, n_t, _ = q.shape
    per_seq = lambda r, width: pl.BlockSpec((1, r, width), lambda *g: (seq_base + step_of(*g[:-1]), 0, 0))
    return dict(
        in_specs=[per_seq(n_t, ATT_W), per_seq(n_t, ATT_W), per_seq(n_t, ATT_W), per_seq(N_HEADS, PAGE)]
                 + [pl.BlockSpec(memory_space=pl.ANY)] * 3,
        args=(q, kn, vn, lfn, ckt, cvt, clft),
        out_spec=pl.BlockSpec((1, n_t, ATT_W), lambda *g: (step_of(*g[:-1]), 0, 0)),
        out_shape=jax.ShapeDtypeStruct((n_steps, n_t, ATT_W), F32),
        scratch=[pltpu.VMEM((2, n_pages, ATT_W, PAGE), F32), pltpu.VMEM((2, n_pages, ATT_W, PAGE), F32),
                 pltpu.VMEM((2, n_pages, N_HEADS, PAGE), F32), pltpu.SemaphoreType.DMA((2, 3))],
        cfg=(layer, n_pages, n_seq, seq_base, n_steps))


def _attn_sample_kernel(pt_ref, *refs, cfg):
    n_in, n_scr = _SideAttention.N_IN, _SideAttention.N_SCRATCH
    side = _SideAttention(pt_ref, refs[:n_in], refs[n_in], refs[n_in + 1:n_in + 1 + n_scr], cfg, pl.program_id(0))
    side.start()
    side.finish()


def _attn_sample(pt_flat, q, kn, vn, lfn, ckt, cvt, clft, *, layer, n_pages):
    n_seq = q.shape[0]
    side = _attn_sample_operands(q, kn, vn, lfn, ckt, cvt, clft, layer=layer, n_pages=n_pages, seq_base=0,
                                 n_steps=n_seq, step_of=lambda s: s)
    return pl.pallas_call(
        functools.partial(_attn_sample_kernel, cfg=side["cfg"]),
        out_shape=side["out_shape"],
        grid_spec=pltpu.PrefetchScalarGridSpec(
            num_scalar_prefetch=1, grid=(n_seq,), in_specs=side["in_specs"], out_specs=side["out_spec"],
            scratch_shapes=side["scratch"]),
        compiler_params=pltpu.CompilerParams(dimension_semantics=("arbitrary",), vmem_limit_bytes=VMEM_LIMIT),
        name="attn_sample",
    )(pt_flat, *side["args"])


def _layer_weights(l, w_in, b_forget, conv_w, pool_w, pool_scale, w_br_attn, w_br_conv, w_br_pool, w_o,
                   ln1_g, ln1_b, w_gate_up, w_down, ln2_g, ln2_b):
    wi = w_in[l]
    row = lambda v: v[l].reshape(1, -1)
    return dict(
        wqkv=wi[:, OFF_Q:OFF_F].astype(BF16),
        wf=jnp.pad(wi[:, OFF_F:OFF_CB], ((0, 0), (0, LANES - N_HEADS))).astype(BF16),
        bf=jnp.pad(row(b_forget), ((0, 0), (0, LANES - N_HEADS))),
        wc=wi[:, OFF_CB:OFF_P].astype(BF16),
        cw=conv_w[l],
        wp=wi[:, OFF_P:OFF_G].astype(BF16),
        pw=pool_w[l].astype(BF16),
        ps=row(pool_scale),
        wg=wi[:, OFF_G:].astype(BF16),
        wbc=w_br_conv[l].astype(BF16),
        wbp=w_br_pool[l].astype(BF16),
        wba=w_br_attn[l].astype(BF16),
        wo=w_o[l].astype(BF16),
        l1g=row(ln1_g), l1b=row(ln1_b),
        wgu=w_gate_up[l].astype(BF16),
        wd=w_down[l].astype(BF16),
        l2g=row(ln2_g), l2b=row(ln2_b),
    )


def kernel(x_prompt, x_sample, cache_k, cache_v, cache_logf, state_conv, state_pool, page_table, c_prompt, c_sample, ada_w, ada_b, w_in, b_forget, conv_w, pool_w, pool_scale, w_br_attn, w_br_conv, w_br_pool, w_o, ln1_g, ln1_b, w_gate_up, w_down, ln2_g, ln2_b):
    depth = ada_w.shape[0]
    alpha = float((2 * depth) ** 0.25)
    nb, seq, d = x_prompt.shape
    n_seq, n_t, _ = x_sample.shape
    n_pool = cache_k.shape[1]
    n_pages = page_table.shape[1]
    past_len = n_pages * PAGE
    tm = 256
    nt = seq // tm

    mod_p, mod_s = _modulation(c_prompt, c_sample, ada_w, ada_b)
    mod_p = mod_p.reshape(depth, nb, 1, 6 * d)

    ckt = cache_k.transpose(0, 1, 3, 4, 2).reshape(depth, n_pool, ATT_W, PAGE)
    cvt = cache_v.transpose(0, 1, 3, 4, 2).reshape(depth, n_pool, ATT_W, PAGE)
    clft = cache_logf.transpose(0, 1, 3, 2)
    pt_flat = page_table.T.reshape(-1)

    xp = x_prompt
    xs = x_sample.transpose(1, 0, 2).reshape(n_t * n_seq, d)
    cprev_t = state_conv.transpose(0, 2, 1, 3)
    pprev_t = state_pool.transpose(0, 2, 1, 3)

    outs_p = [[] for _ in range(3)]
    outs_s = [[] for _ in range(5)]
    kv_all = None
    for l in range(depth):
        w = _layer_weights(l, w_in, b_forget, conv_w, pool_w, pool_scale, w_br_attn, w_br_conv, w_br_pool, w_o,
                           ln1_g, ln1_b, w_gate_up, w_down, ln2_g, ln2_b)
        qs, ks, vs, kts, vts, lfts, g0s, pms, us, ps = _front_sample(
            xs, mod_s[l], cprev_t[l], pprev_t[l], w, n_t=n_t, past_len=past_len)
        seq_major = lambda a: a.reshape(n_t, n_seq, -1).transpose(1, 0, 2)
        lfn = jnp.pad(lfts.transpose(2, 1, 0), ((0, 0), (0, 0), (0, PAGE - n_t)))
        sample_att_in = (seq_major(qs), seq_major(ks), seq_major(vs), lfn, ckt, cvt, clft)
        ride = n_seq == 2 * nb * nt
        side_f = side_b = None
        if ride:
            half = n_seq // 2
            step_of = lambda i, j: i * nt + j
            side_f = _attn_sample_operands(*sample_att_in, layer=l, n_pages=n_pages, seq_base=0, n_steps=half,
                                           step_of=step_of)
            side_b = _attn_sample_operands(*sample_att_in, layer=l, n_pages=n_pages, seq_base=half, n_steps=half,
                                           step_of=step_of)

        front = _front_prompt(xp, mod_p[l], w, kv_all, pt_flat, side_f, tm=tm, layer=l, depth=depth)
        qt, kb, kaux, kt_all, vt_all, lft, drow, g0, pm, cst, pst = front[:11]
        kv_all = (kt_all, vt_all)
        att = _attn_prompt(qt, kb, kaux, vt_all, drow, tq=tm, layer=l)
        back = _back_prompt(xp, att, g0, pm, mod_p[l], w, pt_flat, side_b, tm=tm, alpha=alpha)
        xp = back[0]
        outs_p[0].append(lft.transpose(0, 2, 1))
        outs_p[1].append(cst[:, CONV_HIST - (CONV_K - 1):])
        outs_p[2].append(pst[:, POOL_HIST - POOL_STATE:])

        if ride:
            att_s = jnp.concatenate([front[11], back[1]], axis=0)
        else:
            att_s = _attn_sample(pt_flat, *sample_att_in, layer=l, n_pages=n_pages)
        att_t = att_s.transpose(1, 0, 2).reshape(n_t * n_seq, ATT_W)
        xs = _back_sample(xs, att_t, g0s, pms, mod_s[l], w, pt_flat, steps_per_tile=4, alpha=alpha)
        outs_s[0].append(kts.reshape(n_t, N_HEADS, HEAD_DIM, n_seq).transpose(3, 0, 1, 2))
        outs_s[1].append(vts.reshape(n_t, N_HEADS, HEAD_DIM, n_seq).transpose(3, 0, 1, 2))
        outs_s[2].append(lfts.transpose(2, 0, 1))
        outs_s[3].append(seq_major(us)[:, n_t - (CONV_K - 1):])
        outs_s[4].append(jnp.concatenate([state_pool[l], seq_major(ps)], axis=1)[:, -POOL_STATE:])

    y_sample = xs.reshape(n_t, n_seq, d).transpose(1, 0, 2)
    kv_p = tuple(a.reshape(depth, nb, N_HEADS, HEAD_DIM, seq).transpose(0, 1, 4, 2, 3) for a in kv_all)
    return (xp, y_sample) + kv_p + tuple(jnp.stack(a) for a in outs_p) + tuple(jnp.stack(a) for a in outs_s)
```
